```python
import math
import jax, jax.numpy as jnp
from jax import lax
import numpy as np

D_MODEL = 1024
BATCH = 4
SEQ = 8192
DEPTH = 2
DEC_BATCH = 32
DEC_SEQ = 1
PAST_LEN = 16384
PAGE_SIZE = 128

N_EVEN = (DEPTH + 1) // 2
N_ODD = DEPTH // 2
D_MIX = D_MODEL
D_A = D_MIX // 2
HD_A = 64
H_A = D_A // HD_A
R_W = 64
R_A = 64
D_B = D_MIX - D_A
S5_GROUP = 16
G_B = D_B // S5_GROUP
P_B = 64
D_C = D_MIX // 2
HD_C = 64
H_C = D_C // HD_C
Q_BLOCK = 128
SB_BIAS_HI = 4.0
SB_BIAS_LO = 8.0
D_D = D_MIX - D_C
H_D = 4
DV_D = D_D // H_D
DK_D = DV_D // 2
R_G = 16
GLA_TAU = 16.0
GLA_CHUNK = 64
N_SHIFT = 3 * D_A + R_W + R_A
N_PROJ_E = N_SHIFT + D_A + 2 * D_B
N_PROJ_O = 4 * D_C + 2 * H_D * DK_D + D_D + R_G + D_D
DEEPNORM_ALPHA = (2.0 * DEPTH) ** 0.25
DEEPNORM_BETA = (8.0 * DEPTH) ** -0.25
LN_EPS = 1e-5
GN_EPS = 64e-5
RMS_EPS = 1e-5

kernel_name = 'hybrid_rwkv7_s5_stickbreak_gla_decode_step'


def _offsets(widths):
    out, acc = [], 0
    for w in widths[:-1]:
        acc += w
        out.append(acc)
    return out


def heads(a, hd):
    return a.reshape(a.shape[:-1] + (a.shape[-1] // hd, hd))


def layer_norm(x, g, b):
    xf = x.astype(jnp.float32)
    mu = xf.mean(-1, keepdims=True)
    var = jnp.square(xf - mu).mean(-1, keepdims=True)
    return ((xf - mu) * lax.rsqrt(var + LN_EPS) * g + b).astype(x.dtype)


def rwkv7_scan(r, decay, k, v, aa, bb, s0):
    def step(s, inp):
        r_t, w_t, k_t, v_t, a_t, b_t = inp
        sa = jnp.einsum('bhvk,bhk->bhv', s, a_t)
        s = s * w_t[:, :, None, :] + sa[..., None] * b_t[:, :, None, :] + v_t[..., None] * k_t[:, :, None, :]
        return s, jnp.einsum('bhvk,bhk->bhv', s, r_t)
    xs = tuple(jnp.moveaxis(a.astype(jnp.float32), 1, 0) for a in (r, decay, k, v, aa, bb))
    s_fin, ys = lax.scan(step, s0.astype(jnp.float32), xs)
    return jnp.moveaxis(ys, 0, 1), s_fin


def _cplx_combine(e1, e2):
    a1r, a1i, b1r, b1i = e1
    a2r, a2i, b2r, b2i = e2
    return (a2r * a1r - a2i * a1i,
            a2r * a1i + a2i * a1r,
            a2r * b1r - a2i * b1i + b2r,
            a2r * b1i + a2i * b1r + b2i)


def s5_mixer(u, h0_re, h0_im, lam_re, lam_im, log_dt, b_re, b_im, c_re, c_im, d_skip, glu_w, glu_b):
    f32 = jnp.float32
    bsz, t, _ = u.shape
    dt = jnp.exp(log_dt.astype(f32))[:, None]
    lr, li = lam_re.astype(f32), lam_im.astype(f32)
    mag = jnp.exp(lr * dt)
    abar_re, abar_im = mag * jnp.cos(li * dt), mag * jnp.sin(li * dt)
    den = lr * lr + li * li
    nr, ni = abar_re - 1.0, abar_im
    f_re, f_im = (nr * lr + ni * li) / den, (ni * lr - nr * li) / den
    bb_re = f_re[..., None] * b_re - f_im[..., None] * b_im
    bb_im = f_re[..., None] * b_im + f_im[..., None] * b_re
    ug = u.astype(f32).reshape(bsz, t, G_B, S5_GROUP)
    bu_re = jnp.einsum('btgc,gpc->btgp', ug, bb_re)
    bu_im = jnp.einsum('btgc,gpc->btgp', ug, bb_im)
    a_re = jnp.broadcast_to(abar_re, (1, t, G_B, P_B))
    a_im = jnp.broadcast_to(abar_im, (1, t, G_B, P_B))
    cr, ci, sr, si = lax.associative_scan(_cplx_combine, (a_re, a_im, bu_re, bu_im), axis=1)
    h0r = h0_re.astype(f32)[:, None]
    h0i = h0_im.astype(f32)[:, None]
    h_re = sr + cr * h0r - ci * h0i
    h_im = si + cr * h0i + ci * h0r
    y = jnp.einsum('btgp,gcp->btgc', h_re, c_re) - jnp.einsum('btgp,gcp->btgc', h_im, c_im)
    y = y.reshape(bsz, t, D_B) + d_skip * u.astype(f32)
    y = jax.nn.gelu(y)
    y = y * jax.nn.sigmoid(y @ glu_w + glu_b)
    return y, h_re[:, -1], h_im[:, -1]


def even_layer(x, st_rwkv, st_shift, st_re, st_im, w_in, mu_shift, w0, w2, a0, a2, k_k, k_a, r_k,
               gn_g, gn_b, lam_re, lam_im, log_dt, b_re, b_im, c_re, c_im, d_skip, glu_w, glu_b, w_out):
    f32 = jnp.float32
    bsz, t, _ = x.shape
    proj = jnp.einsum('btd,dn->btn', x, w_in).astype(f32)
    z, gate_a, u, gate_b = jnp.split(proj, _offsets([N_SHIFT, D_A, D_B, D_B]), axis=-1)
    z_prev = jnp.concatenate([st_shift.astype(f32)[:, None, :], z[:, :-1]], axis=1)
    zm = z + mu_shift * (z_prev - z)
    r, k, v, wd, ad = jnp.split(zm, _offsets([D_A, D_A, D_A, R_W, R_A]), axis=-1)
    w = -jax.nn.softplus(-(w0 + jnp.tanh(wd) @ w2)) - 0.5
    decay = jnp.exp(-jnp.exp(w))
    a = jax.nn.sigmoid(a0 + ad @ a2)
    kk = heads(k * k_k, HD_A)
    kk = kk / jnp.maximum(jnp.sqrt(jnp.sum(kk * kk, -1, keepdims=True)), 1e-12)
    k = k * (1.0 + (a - 1.0) * k_a)
    r_h, k_h, v_h, a_h, dec_h = (heads(e, HD_A) for e in (r, k, v, a, decay))
    y, s_fin = rwkv7_scan(r_h, dec_h, k_h, v_h, -kk, kk * a_h, st_rwkv)
    mu = y.mean(-1, keepdims=True)
    var = jnp.square(y - mu).mean(-1, keepdims=True)
    y = ((y - mu) * lax.rsqrt(var + GN_EPS)).reshape(bsz, t, D_A) * gn_g + gn_b
    y = y + (jnp.sum(r_h * k_h * r_k, -1, keepdims=True) * v_h).reshape(bsz, t, D_A)
    out_a = y * jax.nn.silu(gate_a)
    y_b, h_re, h_im = s5_mixer(u, st_re, st_im, lam_re, lam_im, log_dt, b_re, b_im, c_re, c_im, d_skip, glu_w, glu_b)
    out_b = y_b * jax.nn.silu(gate_b)
    out = jnp.concatenate([out_a, out_b], axis=-1).astype(x.dtype) @ w_out
    return out, s_fin, z[:, -1], h_re, h_im


def stick_breaking(q, k, v, bias, q_offset):
    f32 = jnp.float32
    bsz, tq, h, d = q.shape
    tk = k.shape[1]
    qb = math.gcd(Q_BLOCK, tq)
    nb = tq // qb
    qf = jnp.moveaxis((q.astype(f32) * d ** -0.5).reshape(bsz, nb, qb, h, d), 1, 0)
    kf, vf = k.astype(f32), v.astype(f32)
    bf = bias.astype(f32)[None, :, None, None]
    key_pos = jnp.arange(tk)

    def block(args):
        qblk, i = args
        qpos = q_offset + i * qb + jnp.arange(qb)
        z = jnp.einsum('bqhd,bkhd->bhqk', qblk, kf) + bf
        causal = key_pos[None, :] < qpos[:, None]
        log_1mb = jnp.where(causal, jax.nn.log_sigmoid(-z), 0.0)
        suffix = lax.cumsum(log_1mb, axis=3, reverse=True)
        excl = jnp.concatenate([suffix[..., 1:], jnp.zeros_like(suffix[..., :1])], axis=-1)
        att = jnp.where(causal, jnp.exp(jax.nn.log_sigmoid(z) + excl), 0.0)
        return jnp.einsum('bhqk,bkhd->bqhd', att, vf)

    out = lax.map(block, (qf, jnp.arange(nb)))
    return jnp.moveaxis(out, 0, 1).reshape(bsz, tq, h, d)


def gla_chunked(q, k, v, log_a, s0):
    f32 = jnp.float32
    bsz, t, h, _ = q.shape
    dv = v.shape[-1]
    c = math.gcd(GLA_CHUNK, t)
    n = t // c

    def to_chunks(a):
        return jnp.moveaxis(a.astype(f32).reshape(bsz, n, c, h, a.shape[-1]), 1, 0)

    qc, kc, vc, gc = (to_chunks(e) for e in (q, k, v, log_a))
    mask = jnp.tril(jnp.ones((c, c), dtype=bool))[None, :, :, None, None]

    def step(s, inp):
        qi, ki, vi, gi = inp
        b = jnp.cumsum(gi, axis=1)
        inter = jnp.einsum('bthk,bhkv->bthv', qi * jnp.exp(b), s)
        diff = jnp.where(mask, b[:, :, None] - b[:, None, :], -jnp.inf)
        scores = jnp.einsum('bthk,bshk,btshk->bhts', qi, ki, jnp.exp(diff))
        intra = jnp.einsum('bhts,bshv->bthv', scores, vi)
        b_last = b[:, -1]
        k_dec = ki * jnp.exp(b_last[:, None] - b)
        s_new = jnp.exp(b_last)[..., None] * s + jnp.einsum('bshk,bshv->bhkv', k_dec, vi)
        return s_new, inter + intra

    s_fin, out = lax.scan(step, s0.astype(f32), (qc, kc, vc, gc))
    return jnp.moveaxis(out, 0, 1).reshape(bsz, t, h, dv), s_fin


def odd_layer(x, k_past, v_past, st_gla, w_in, sb_bias, alpha_w, alpha_b, gn_g, w_out):
    f32 = jnp.float32
    bsz, t, _ = x.shape
    proj = jnp.einsum('btd,dn->btn', x, w_in)
    q_c, k_c, v_c, g_c, q_d, k_d, v_d, a_d, g_d = jnp.split(
        proj, _offsets([D_C, D_C, D_C, D_C, H_D * DK_D, H_D * DK_D, D_D, R_G, D_D]), axis=-1)
    k_new, v_new = heads(k_c, HD_C), heads(v_c, HD_C)
    if k_past is None:
        k_all, v_all, offset = k_new, v_new, 0
    else:
        k_all = jnp.concatenate([k_past.astype(k_new.dtype), k_new], axis=1)
        v_all = jnp.concatenate([v_past.astype(v_new.dtype), v_new], axis=1)
        offset = k_past.shape[1]
    o_c = stick_breaking(heads(q_c, HD_C), k_all, v_all, sb_bias, offset).reshape(bsz, t, D_C)
    log_a = jax.nn.log_sigmoid((a_d @ alpha_w + alpha_b).astype(f32)) / GLA_TAU
    o_d, s_fin = gla_chunked(heads(q_d.astype(f32) * DK_D ** -0.5, DK_D), heads(k_d, DK_D),
                             heads(v_d, DV_D), heads(log_a, DK_D), st_gla)
    o_d = (o_d * lax.rsqrt(jnp.mean(o_d * o_d, -1, keepdims=True) + RMS_EPS)).reshape(bsz, t, D_D) * gn_g
    out = jnp.concatenate([o_c * jax.nn.silu(g_c.astype(f32)), o_d * jax.nn.silu(g_d.astype(f32))], axis=-1)
    return out.astype(x.dtype) @ w_out, k_new, v_new, s_fin


def setup_inputs(seed: int = 0) -> dict:
    key = jax.random.key(seed)
    ks = iter(jax.random.split(key, 48))
    f32 = jnp.float32

    def nrm(shape, scale=1.0):
        return jax.random.normal(next(ks), shape, f32) * scale

    n_pages = PAST_LEN // PAGE_SIZE
    n_phys = (DEC_BATCH * n_pages * 5) // 4
    x_prompt = nrm((BATCH, SEQ, D_MODEL))
    x_sample = nrm((DEC_BATCH, DEC_SEQ, D_MODEL))
    state_rwkv = nrm((N_EVEN, DEC_BATCH, H_A, HD_A, HD_A), 0.5)
    state_shift = nrm((N_EVEN, DEC_BATCH, N_SHIFT))
    state_s5_re = nrm((N_EVEN, DEC_BATCH, G_B, P_B), 0.3)
    state_s5_im = nrm((N_EVEN, DEC_BATCH, G_B, P_B), 0.3)
    cache_k = nrm((N_ODD, n_phys, PAGE_SIZE, H_C, HD_C))
    cache_v = nrm((N_ODD, n_phys, PAGE_SIZE, H_C, HD_C))
    state_gla = nrm((N_ODD, DEC_BATCH, H_D, DK_D, DV_D), 0.3)
    page_table = jax.random.permutation(next(ks), n_phys)[: DEC_BATCH * n_pages].reshape(DEC_BATCH, n_pages).astype(jnp.int32)
    ln_g = 1.0 + nrm((DEPTH, D_MODEL), 0.05)
    ln_b = nrm((DEPTH, D_MODEL), 0.01)
    w_in_e = nrm((N_EVEN, D_MODEL, N_PROJ_E), D_MODEL ** -0.5)
    mu_shift = jax.random.uniform(next(ks), (N_EVEN, N_SHIFT), f32)
    ramp = jnp.arange(D_A, dtype=f32) / (D_A - 1)
    rwkv_w0 = -6.0 + 5.0 * ramp ** 1.5 + nrm((N_EVEN, D_A), 0.1)
    rwkv_w2 = nrm((N_EVEN, R_W, D_A), 0.1 * R_W ** -0.5)
    rwkv_a0 = nrm((N_EVEN, D_A), 0.1)
    rwkv_a2 = nrm((N_EVEN, R_A, D_A), 0.1 * R_A ** -0.5)
    rwkv_k_k = 0.85 + nrm((N_EVEN, D_A), 0.05)
    rwkv_k_a = 1.0 + nrm((N_EVEN, D_A), 0.05)
    rwkv_r_k = nrm((N_EVEN, H_A, HD_A), 0.1)
    rwkv_gn_g = 1.0 + nrm((N_EVEN, D_A), 0.05)
    rwkv_gn_b = nrm((N_EVEN, D_A), 0.01)
    s5_lambda_re = -0.5 + nrm((N_EVEN, G_B, P_B), 0.01)
    s5_lambda_im = math.pi * jnp.arange(P_B, dtype=f32) + nrm((N_EVEN, G_B, P_B), 0.01)
    s5_log_dt = jax.random.uniform(next(ks), (N_EVEN, G_B), f32, math.log(0.001), math.log(0.1))
    s5_b_re = nrm((N_EVEN, G_B, P_B, S5_GROUP), (2.0 * S5_GROUP) ** -0.5)
    s5_b_im = nrm((N_EVEN, G_B, P_B, S5_GROUP), (2.0 * S5_GROUP) ** -0.5)
    s5_c_re = nrm((N_EVEN, G_B, S5_GROUP, P_B), (2.0 * P_B) ** -0.5)
    s5_c_im = nrm((N_EVEN, G_B, S5_GROUP, P_B), (2.0 * P_B) ** -0.5)
    s5_d = nrm((N_EVEN, D_B))
    s5_glu_w = nrm((N_EVEN, D_B, D_B), D_B ** -0.5)
    s5_glu_b = nrm((N_EVEN, D_B), 0.01)
    w_out_e = nrm((N_EVEN, D_MIX, D_MODEL), DEEPNORM_BETA * D_MIX ** -0.5)
    w_in_o = nrm((N_ODD, D_MODEL, N_PROJ_O), D_MODEL ** -0.5)
    sb_bias = -jnp.linspace(SB_BIAS_HI, SB_BIAS_LO, H_C, dtype=f32)[None, :] + nrm((N_ODD, H_C), 0.05)
    gla_alpha_w = nrm((N_ODD, R_G, H_D * DK_D), R_G ** -0.5)
    gla_alpha_b = nrm((N_ODD, H_D * DK_D), 0.5)
    gla_gn_g = 1.0 + nrm((N_ODD, D_D), 0.05)
    w_out_o = nrm((N_ODD, D_MIX, D_MODEL), DEEPNORM_BETA * D_MIX ** -0.5)
    return {'x_prompt': x_prompt, 'x_sample': x_sample, 'state_rwkv': state_rwkv, 'state_shift': state_shift,
            'state_s5_re': state_s5_re, 'state_s5_im': state_s5_im, 'cache_k': cache_k, 'cache_v': cache_v,
            'state_gla': state_gla, 'page_table': page_table, 'ln_g': ln_g, 'ln_b': ln_b,
            'w_in_e': w_in_e, 'mu_shift': mu_shift, 'rwkv_w0': rwkv_w0, 'rwkv_w2': rwkv_w2,
            'rwkv_a0': rwkv_a0, 'rwkv_a2': rwkv_a2, 'rwkv_k_k': rwkv_k_k, 'rwkv_k_a': rwkv_k_a,
            'rwkv_r_k': rwkv_r_k, 'rwkv_gn_g': rwkv_gn_g, 'rwkv_gn_b': rwkv_gn_b,
            's5_lambda_re': s5_lambda_re, 's5_lambda_im': s5_lambda_im, 's5_log_dt': s5_log_dt,
            's5_b_re': s5_b_re, 's5_b_im': s5_b_im, 's5_c_re': s5_c_re, 's5_c_im': s5_c_im,
            's5_d': s5_d, 's5_glu_w': s5_glu_w, 's5_glu_b': s5_glu_b, 'w_out_e': w_out_e,
            'w_in_o': w_in_o, 'sb_bias': sb_bias, 'gla_alpha_w': gla_alpha_w, 'gla_alpha_b': gla_alpha_b,
            'gla_gn_g': gla_gn_g, 'w_out_o': w_out_o}


def reference(x_prompt, x_sample, state_rwkv, state_shift, state_s5_re, state_s5_im, cache_k, cache_v,
              state_gla, page_table, ln_g, ln_b, w_in_e, mu_shift, rwkv_w0, rwkv_w2, rwkv_a0, rwkv_a2,
              rwkv_k_k, rwkv_k_a, rwkv_r_k, rwkv_gn_g, rwkv_gn_b, s5_lambda_re, s5_lambda_im, s5_log_dt,
              s5_b_re, s5_b_im, s5_c_re, s5_c_im, s5_d, s5_glu_w, s5_glu_b, w_out_e,
              w_in_o, sb_bias, gla_alpha_w, gla_alpha_b, gla_gn_g, w_out_o):
    f32 = jnp.float32
    bp = x_prompt.shape[0]
    bs = x_sample.shape[0]
    n_pages = page_table.shape[1]
    xp, xs = x_prompt, x_sample
    rw_p, rw_s, sh_p, sh_s, re_p, re_s, im_p, im_s = [], [], [], [], [], [], [], []
    kr_p, kr_s, vr_p, vr_s, gl_p, gl_s = [], [], [], [], [], []
    for l in range(DEPTH):
        i = l // 2
        if l % 2 == 0:
            wts = (w_in_e[i], mu_shift[i], rwkv_w0[i], rwkv_w2[i], rwkv_a0[i], rwkv_a2[i], rwkv_k_k[i],
                   rwkv_k_a[i], rwkv_r_k[i], rwkv_gn_g[i], rwkv_gn_b[i], s5_lambda_re[i], s5_lambda_im[i],
                   s5_log_dt[i], s5_b_re[i], s5_b_im[i], s5_c_re[i], s5_c_im[i], s5_d[i], s5_glu_w[i],
                   s5_glu_b[i], w_out_e[i])
            yp, a1, a2, a3, a4 = even_layer(
                xp, jnp.zeros((bp, H_A, HD_A, HD_A), f32), jnp.zeros((bp, N_SHIFT), f32),
                jnp.zeros((bp, G_B, P_B), f32), jnp.zeros((bp, G_B, P_B), f32), *wts)
            ys, b1, b2, b3, b4 = even_layer(xs, state_rwkv[i], state_shift[i], state_s5_re[i], state_s5_im[i], *wts)
            rw_p.append(a1); sh_p.append(a2); re_p.append(a3); im_p.append(a4)
            rw_s.append(b1); sh_s.append(b2); re_s.append(b3); im_s.append(b4)
        else:
            wts = (w_in_o[i], sb_bias[i], gla_alpha_w[i], gla_alpha_b[i], gla_gn_g[i], w_out_o[i])
            yp, k1, v1, g1 = odd_layer(xp, None, None, jnp.zeros((bp, H_D, DK_D, DV_D), f32), *wts)
            k_past = cache_k[i][page_table].reshape(bs, n_pages * PAGE_SIZE, H_C, HD_C)
            v_past = cache_v[i][page_table].reshape(bs, n_pages * PAGE_SIZE, H_C, HD_C)
            ys, k2, v2, g2 = odd_layer(xs, k_past, v_past, state_gla[i], *wts)
            kr_p.append(k1); vr_p.append(v1); gl_p.append(g1)
            kr_s.append(k2); vr_s.append(v2); gl_s.append(g2)
        xp = layer_norm(DEEPNORM_ALPHA * xp + yp, ln_g[l], ln_b[l])
        xs = layer_norm(DEEPNORM_ALPHA * xs + ys, ln_g[l], ln_b[l])
    return (xp, xs,
            jnp.stack(rw_p), jnp.stack(rw_s), jnp.stack(sh_p), jnp.stack(sh_s),
            jnp.stack(re_p), jnp.stack(re_s), jnp.stack(im_p), jnp.stack(im_s),
            jnp.stack(kr_p), jnp.stack(kr_s), jnp.stack(vr_p), jnp.stack(vr_s),
            jnp.stack(gl_p), jnp.stack(gl_s))
```

```python
import functools
import math

import numpy as np
import jax
import jax.numpy as jnp
from jax import lax
from jax.experimental import pallas as pl
from jax.experimental.pallas import tpu as pltpu

F32 = jnp.float32
BF16 = jnp.bfloat16

D_MODEL = 1024
D_A = 512
HD_A = 64
H_A = 8
R_W = 64
R_A = 64
D_B = 512
S5_GROUP = 16
G_B = 32
P_B = 64
N_STATE_B = G_B * P_B
D_C = 512
HD_C = 64
H_C = 8
D_D = 512
H_D = 4
DV_D = 128
DK_D = 64
R_G = 16
GLA_TAU = 16.0
N_SHIFT = 3 * D_A + R_W + R_A
N_PROJ_E = N_SHIFT + D_A + 2 * D_B
DEPTH = 2
DEEPNORM_ALPHA = (2.0 * DEPTH) ** 0.25
LN_EPS = 1e-5
GN_EPS = 64e-5
RMS_EPS = 1e-5
PAGE = 128

LANES = 128
VMEM_LIMIT = 56 * 1024 * 1024

PROJ_BLOCK = 512
S5_BLOCK = 256
RWKV_CHUNK = 64
GLA_CHUNK = 64
DECODE_CHUNK = 8
SB_TILE = 256
PAGES_PER_STEP = 4


def _cparams(*sem):
    return pltpu.CompilerParams(dimension_semantics=sem, vmem_limit_bytes=VMEM_LIMIT)


def _mm(a, b):
    return jnp.dot(a.astype(BF16), b.astype(BF16), preferred_element_type=F32)


def _mm_nt(a, b):
    return lax.dot_general(a.astype(BF16), b.astype(BF16), (((1,), (1,)), ((), ())),
                           preferred_element_type=F32)


def _mm_tn(a, b):
    return lax.dot_general(a.astype(BF16), b.astype(BF16), (((0,), (0,)), ((), ())),
                           preferred_element_type=F32)


def _mm_f32(a, b):
    return jnp.dot(a, b, precision=lax.Precision.HIGHEST, preferred_element_type=F32)


def _mmw(a, w):
    if w.dtype == BF16:
        return jnp.dot(a.astype(BF16), w, preferred_element_type=F32)
    return _mm_f32(a, w)


def _split(x, parts):
    out = []
    rem = x
    for _ in range(parts - 1):
        p = rem.astype(BF16)
        out.append(p)
        rem = rem - p.astype(F32)
    out.append(rem.astype(BF16))
    return out


def _sel_mm(sel, x, parts):
    acc = None
    for p in _split(x, parts):
        t = jnp.dot(sel, p, preferred_element_type=F32)
        acc = t if acc is None else acc + t
    return acc


def _mm_sel(x, sel, parts):
    acc = None
    for p in _split(x, parts):
        t = jnp.dot(p, sel, preferred_element_type=F32)
        acc = t if acc is None else acc + t
    return acc


def _sigmoid(t):
    return 1.0 / (1.0 + jnp.exp(-t))


def _silu(t):
    return t * _sigmoid(t)


def _log1pexp_negabs(t):
    return jnp.log1p(jnp.exp(-jnp.abs(t)))


def _softplus(t):
    return jnp.maximum(t, 0.0) + _log1pexp_negabs(t)


def _gelu_tanh(x):
    return 0.5 * x * (1.0 + jnp.tanh(math.sqrt(2.0 / math.pi) * (x + 0.044715 * (x * x * x))))


def _tri_incl(n):
    return np.tril(np.ones((n, n), np.float32))


def _block_ones(n, blk):
    idx = np.arange(n) // blk
    return (idx[:, None] == idx[None, :]).astype(np.float32)


def _gla_decay_rows(n):
    t = np.arange(n)[:, None]
    i = np.arange(n)[None, :]
    mats = [(i <= t), (i > t)]
    m = n // 2
    while m >= 1:
        p = 2 * m * (t // (2 * m)) + m - 1
        upper = (t % (2 * m)) >= m
        mats.append(np.where(upper, (i > p) & (i <= t), (i > t) & (i <= p)))
        m //= 2
    return np.concatenate([x.astype(np.float32) for x in mats], axis=0)


def _gla_levels(n):
    out = []
    m = n // 2
    while m >= 1:
        out.append(m)
        m //= 2
    return out


def _even_prep_body(decode, x_ref, zp_ref, w_ref, mu_ref, w0_ref, w2_ref, a0_ref, a2_ref, kk_ref, ka_ref,
                    ones_ref, r_o, lw_o, k_o, v_o, aa_o, bb_o, ga_o, u_o, gb_o, sh_o, carry):
    x = x_ref[0].astype(BF16)
    proj = jnp.dot(x, w_ref[...], preferred_element_type=F32)
    z = proj[:, :N_SHIFT]
    tb = z.shape[0]
    if decode:
        zp = zp_ref[0]
        sh_o[0] = z
    else:
        @pl.when(pl.program_id(1) == 0)
        def _():
            carry[...] = zp_ref[0]
        row = lax.broadcasted_iota(jnp.int32, z.shape, 0)
        zp = jnp.where(row == 0, carry[...], pltpu.roll(z, 1, 0))
        carry[...] = z[tb - 1:tb, :]
        sh_o[0] = z[tb - 1:tb, :]
    zm = z + mu_ref[...] * (zp - z)
    r = zm[:, 0:D_A]
    k = zm[:, D_A:2 * D_A]
    v = zm[:, 2 * D_A:3 * D_A]
    wd = zm[:, 3 * D_A:3 * D_A + R_W]
    ad = zm[:, 3 * D_A + R_W:N_SHIFT]
    w = -_softplus(-(w0_ref[...] + _mm_f32(jnp.tanh(wd), w2_ref[...]))) - 0.5
    a = _sigmoid(a0_ref[...] + _mm_f32(ad, a2_ref[...]))
    kk = k * kk_ref[...]
    ss = _mm_sel(kk * kk, ones_ref[...], 2)
    kkn = kk / jnp.maximum(jnp.sqrt(ss), 1e-12)
    r_o[0] = r
    lw_o[0] = -jnp.exp(w)
    k_o[0] = k * (1.0 + (a - 1.0) * ka_ref[...])
    v_o[0] = v
    aa_o[0] = -kkn
    bb_o[0] = kkn * a
    ga_o[0] = proj[:, N_SHIFT:N_SHIFT + D_A]
    u_o[0] = proj[:, N_SHIFT + D_A:N_SHIFT + D_A + D_B]
    gb_o[0] = proj[:, N_SHIFT + D_A + D_B:N_PROJ_E]


def _even_prep(x, zprev, w_bf, mu, w0, w2, a0, a2, k_k, k_a, decode):
    bsz, t, _ = x.shape
    tb = t if decode else min(PROJ_BLOCK, t)
    nt = t // tb
    row2 = lambda a: a.reshape(1, -1)
    ones = jnp.asarray(_block_ones(D_A, HD_A), BF16)
    full = lambda shape: pl.BlockSpec(shape, lambda b, i: (0,) * len(shape))
    tok = lambda n: pl.BlockSpec((1, tb, n), lambda b, i: (b, i, 0))
    zp_spec = (pl.BlockSpec((1, tb, N_SHIFT), lambda b, i: (b, i, 0)) if decode
               else pl.BlockSpec((1, 1, N_SHIFT), lambda b, i: (b, 0, 0)))
    sh_rows = tb if decode else 1
    outs = [jax.ShapeDtypeStruct((bsz, t, D_A), F32)] * 9 + [jax.ShapeDtypeStruct((bsz, sh_rows, N_SHIFT), F32)]
    out_specs = [tok(D_A)] * 9 + [pl.BlockSpec((1, sh_rows, N_SHIFT), lambda b, i: (b, 0, 0))]
    return pl.pallas_call(
        functools.partial(_even_prep_body, decode),
        grid=(bsz, nt),
        in_specs=[tok(D_MODEL), zp_spec, full((D_MODEL, N_PROJ_E)), full((1, N_SHIFT)), full((1, D_A)),
                  full((R_W, D_A)), full((1, D_A)), full((R_A, D_A)), full((1, D_A)), full((1, D_A)),
                  full((D_A, D_A))],
        out_specs=out_specs,
        out_shape=outs,
        scratch_shapes=[pltpu.VMEM((1, N_SHIFT), F32)],
        compiler_params=_cparams("arbitrary", "arbitrary"),
        name="even_prep_decode" if decode else "even_prep",
    )(x, zprev, w_bf, row2(mu), row2(w0), w2, row2(a0), a2, row2(k_k), row2(k_a), ones)


def _rwkv_body(r_ref, lw_ref, k_ref, v_ref, aa_ref, bb_ref, ga_ref, s0_ref, tri_ref, gng_ref, gnb_ref, rk_ref,
               oa_o, sf_o, state):
    t_idx = pl.program_id(1)

    @pl.when(t_idx == 0)
    def _():
        state[...] = s0_ref[0]

    lw = lw_ref[0]
    n = lw.shape[0]
    c = _sel_mm(tri_ref[...], lw, 3)
    c_last = c[n - 1:n, :]
    r = r_ref[0]
    k = k_ref[0]
    v = v_ref[0]
    bb = bb_ref[0]
    pinv = jnp.exp(-c)
    a_t = aa_ref[0] * jnp.exp(c - lw)
    b_t = bb * pinv
    k_t = k * pinv
    r_t = r * jnp.exp(c)
    tail = jnp.exp(c_last - c)
    b_l = bb * tail
    k_l = k * tail
    p_last = jnp.exp(c_last)
    rkr = r * k * rk_ref[...]

    row = lax.broadcasted_iota(jnp.int32, (n, n), 0)
    col = lax.broadcasted_iota(jnp.int32, (n, n), 1)
    strict = col < row
    incl = col <= row
    levels = int(math.log2(n))

    ys = []
    for h in range(H_A):
        sl = slice(HD_A * h, HD_A * (h + 1))
        a_h, b_h, k_h, r_h, v_h = a_t[:, sl], b_t[:, sl], k_t[:, sl], r_t[:, sl], v[:, sl]
        s_h = state[h]
        nmat = jnp.where(strict, _mm_nt(a_h, b_h), 0.0)
        mmat = jnp.where(strict, _mm_nt(a_h, k_h), 0.0)
        u = _mm_nt(a_h, s_h) + _mm(mmat, v_h)
        npow = nmat
        for lvl in range(levels):
            u = u + _mm(npow, u)
            if lvl + 1 < levels:
                npow = _mm(npow, npow)
        y = (_mm_nt(r_h, s_h) + _mm(jnp.where(incl, _mm_nt(r_h, b_h), 0.0), u)
             + _mm(jnp.where(incl, _mm_nt(r_h, k_h), 0.0), v_h))
        state[h] = s_h * p_last[:, sl] + _mm_tn(u, b_l[:, sl]) + _mm_tn(v_h, k_l[:, sl])
        mu = jnp.mean(y, axis=-1, keepdims=True)
        var = jnp.mean(jnp.square(y - mu), axis=-1, keepdims=True)
        bonus = jnp.sum(rkr[:, sl], axis=-1, keepdims=True) * v_h
        ys.append(((y - mu) * lax.rsqrt(var + GN_EPS), bonus))
    yn = jnp.concatenate([p[0] for p in ys], axis=-1)
    bonus = jnp.concatenate([p[1] for p in ys], axis=-1)
    oa_o[0] = (yn * gng_ref[...] + gnb_ref[...] + bonus) * _silu(ga_ref[0])

    @pl.when(t_idx == pl.num_programs(1) - 1)
    def _():
        sf_o[0] = state[...]


def _rwkv_scan(r, lw, k, v, aa, bb, ga, s0, gn_g, gn_b, r_k, chunk):
    bsz, t, _ = r.shape
    nt = t // chunk
    tok = pl.BlockSpec((1, chunk, D_A), lambda b, i: (b, i, 0))
    st = pl.BlockSpec((1, H_A, HD_A, HD_A), lambda b, i: (b, 0, 0, 0))
    full = lambda shape: pl.BlockSpec(shape, lambda b, i: (0,) * len(shape))
    tri = jnp.asarray(_tri_incl(chunk), BF16)
    return pl.pallas_call(
        _rwkv_body,
        grid=(bsz, nt),
        in_specs=[tok] * 7 + [st, full((chunk, chunk)), full((1, D_A)), full((1, D_A)), full((1, D_A))],
        out_specs=[tok, st],
        out_shape=[jax.ShapeDtypeStruct((bsz, t, D_A), F32),
                   jax.ShapeDtypeStruct((bsz, H_A, HD_A, HD_A), F32)],
        scratch_shapes=[pltpu.VMEM((H_A, HD_A, HD_A), F32)],
        compiler_params=_cparams("arbitrary", "arbitrary"),
        name="rwkv_scan_c%d" % chunk,
    )(r, lw, k, v, aa, bb, ga, s0, tri, gn_g.reshape(1, -1), gn_b.reshape(1, -1), r_k.reshape(1, -1))


S5_LANE_BLOCKS = D_B // LANES
S5_STATES_PER_BLOCK = N_STATE_B // S5_LANE_BLOCKS


def _s5_body(decode, u_ref, gb_ref, h0r_ref, h0i_ref, ar_ref, ai_ref, bbr_ref, bbi_ref, cr_ref, ci_ref, d_ref,
             gw_ref, gbias_ref, ob_o, hr_o, hi_o, car_r, car_i):
    u = u_ref[0]
    tb = u.shape[0]
    if not decode:
        @pl.when(pl.program_id(1) == 0)
        def _():
            car_r[...] = h0r_ref[0]
            car_i[...] = h0i_ref[0]
        row = lax.broadcasted_iota(jnp.int32, (tb, S5_STATES_PER_BLOCK), 0)
    ys = []
    for j in range(S5_LANE_BLOCKS):
        sl = slice(S5_STATES_PER_BLOCK * j, S5_STATES_PER_BLOCK * (j + 1))
        uj = u[:, LANES * j:LANES * (j + 1)]
        bur = _mmw(uj, bbr_ref[j])
        bui = _mmw(uj, bbi_ref[j])
        ar = ar_ref[:, sl]
        ai = ai_ref[:, sl]
        if decode:
            h0r = h0r_ref[0][:, sl]
            h0i = h0i_ref[0][:, sl]
            hr = ar * h0r - ai * h0i + bur
            hi = ar * h0i + ai * h0r + bui
            hr_o[0, :, sl] = hr
            hi_o[0, :, sl] = hi
        else:
            cr = car_r[:, sl]
            ci = car_i[:, sl]
            first = row == 0
            hr = bur + jnp.where(first, ar * cr - ai * ci, 0.0)
            hi = bui + jnp.where(first, ar * ci + ai * cr, 0.0)
            pr, pi = ar, ai
            s = 1
            while s < tb:
                keep = row >= s
                sr = jnp.where(keep, pltpu.roll(hr, s, 0), 0.0)
                si = jnp.where(keep, pltpu.roll(hi, s, 0), 0.0)
                hr, hi = hr + pr * sr - pi * si, hi + pr * si + pi * sr
                pr, pi = pr * pr - pi * pi, 2.0 * pr * pi
                s *= 2
            car_r[:, sl] = hr[tb - 1:tb, :]
            car_i[:, sl] = hi[tb - 1:tb, :]
            hr_o[0, :, sl] = hr[tb - 1:tb, :]
            hi_o[0, :, sl] = hi[tb - 1:tb, :]
        ys.append(_mm(hr, cr_ref[j]) - _mm(hi, ci_ref[j]))
    y = jnp.concatenate(ys, axis=-1) + d_ref[...] * u
    y = _gelu_tanh(y)
    y = y * _sigmoid(_mm(y, gw_ref[...]) + gbias_ref[...])
    ob_o[0] = y * _silu(gb_ref[0])


def _s5_mixer(u, gate_b, h0r, h0i, abar_re, abar_im, bbr, bbi, cr, ci, d_skip, glu_w, glu_b, decode):
    bsz, t, _ = u.shape
    tb = t if decode else min(S5_BLOCK, t)
    nt = t // tb
    st_rows = tb if decode else 1
    tok = pl.BlockSpec((1, tb, D_B), lambda b, i: (b, i, 0))
    st = pl.BlockSpec((1, st_rows, N_STATE_B), lambda b, i: (b, 0, 0))
    full = lambda shape: pl.BlockSpec(shape, lambda b, i: (0,) * len(shape))
    return pl.pallas_call(
        functools.partial(_s5_body, decode),
        grid=(bsz, nt),
        in_specs=[tok, tok, st, st, full((1, N_STATE_B)), full((1, N_STATE_B)),
                  full(bbr.shape), full(bbi.shape), full(cr.shape), full(ci.shape),
                  full((1, D_B)), full((D_B, D_B)), full((1, D_B))],
        out_specs=[tok, st, st],
        out_shape=[jax.ShapeDtypeStruct((bsz, t, D_B), F32),
                   jax.ShapeDtypeStruct((bsz, st_rows, N_STATE_B), F32),
                   jax.ShapeDtypeStruct((bsz, st_rows, N_STATE_B), F32)],
        scratch_shapes=[pltpu.VMEM((1, N_STATE_B), F32), pltpu.VMEM((1, N_STATE_B), F32)],
        compiler_params=_cparams("arbitrary", "arbitrary"),
        name="s5_decode" if decode else "s5_scan",
    )(u, gate_b, h0r, h0i, abar_re, abar_im, bbr, bbi, cr, ci, d_skip.reshape(1, -1), glu_w,
      glu_b.reshape(1, -1))


def _s5_params(lam_re, lam_im, log_dt, b_re, b_im, c_re, c_im):
    dt = jnp.exp(log_dt)[:, None]
    mag = jnp.exp(lam_re * dt)
    abar_re, abar_im = mag * jnp.cos(lam_im * dt), mag * jnp.sin(lam_im * dt)
    den = lam_re * lam_re + lam_im * lam_im
    nr, ni = abar_re - 1.0, abar_im
    f_re, f_im = (nr * lam_re + ni * lam_im) / den, (ni * lam_re - nr * lam_im) / den
    bb_re = f_re[..., None] * b_re - f_im[..., None] * b_im
    bb_im = f_re[..., None] * b_im + f_im[..., None] * b_re
    gpb = G_B // S5_LANE_BLOCKS
    eye = jnp.eye(gpb, dtype=F32)

    def stage_b(m):
        m = m.reshape(S5_LANE_BLOCKS, gpb, P_B, S5_GROUP)
        blk = jnp.einsum('jgpc,gh->jgchp', m, eye)
        return blk.reshape(S5_LANE_BLOCKS, gpb * S5_GROUP, gpb * P_B)

    def stage_c(m):
        m = m.reshape(S5_LANE_BLOCKS, gpb, S5_GROUP, P_B)
        blk = jnp.einsum('jgcp,gh->jgphc', m, eye)
        return blk.reshape(S5_LANE_BLOCKS, gpb * P_B, gpb * S5_GROUP)

    return (abar_re.reshape(1, -1), abar_im.reshape(1, -1), stage_b(bb_re), stage_b(bb_im),
            stage_c(c_re), stage_c(c_im))


def _out_norm_body(a_ref, b_ref, x_ref, w_ref, g_ref, beta_ref, o_ref):
    half = a_ref.shape[-1]
    out = _mmw(a_ref[0], w_ref[0:half, :]) + _mmw(b_ref[0], w_ref[half:2 * half, :])
    h = DEEPNORM_ALPHA * x_ref[0] + out
    mu = jnp.mean(h, axis=-1, keepdims=True)
    var = jnp.mean(jnp.square(h - mu), axis=-1, keepdims=True)
    o_ref[0] = (h - mu) * lax.rsqrt(var + LN_EPS) * g_ref[...] + beta_ref[...]


def _out_norm(a, b, x, w_bf, g, beta):
    bsz, t, half = a.shape
    tb = min(PROJ_BLOCK, t)
    nt = t // tb
    tok = lambda n: pl.BlockSpec((1, tb, n), lambda bi, i: (bi, i, 0))
    full = lambda shape: pl.BlockSpec(shape, lambda bi, i: (0,) * len(shape))
    return pl.pallas_call(
        _out_norm_body,
        grid=(bsz, nt),
        in_specs=[tok(half), tok(half), tok(D_MODEL), full((2 * half, D_MODEL)), full((1, D_MODEL)),
                  full((1, D_MODEL))],
        out_specs=tok(D_MODEL),
        out_shape=jax.ShapeDtypeStruct((bsz, t, D_MODEL), F32),
        compiler_params=_cparams("arbitrary", "arbitrary"),
        name="out_norm",
    )(a, b, x, w_bf, g.reshape(1, -1), beta.reshape(1, -1))


O_QC, O_KC, O_VC, O_GC = 0, 512, 1024, 1536
O_QD, O_KD, O_VD, O_GD, O_AD = 2048, 2304, 2560, 3072, 3584
N_PROJ_O_STAGED = 3712


def _odd_prep_body(x_ref, w_ref, aw_ref, ab_ref, qc_o, kc_o, vc_o, kcb_o, vcb_o, gc_o, qd_o, kd_o, vd_o, la_o,
                   gd_o):
    x = x_ref[0].astype(BF16)
    proj = jnp.dot(x, w_ref[...], preferred_element_type=F32)
    kc = proj[:, O_KC:O_VC]
    vc = proj[:, O_VC:O_GC]
    qc_o[0] = (proj[:, O_QC:O_KC] * (HD_C ** -0.5)).astype(BF16)
    kc_o[0] = kc
    vc_o[0] = vc
    kcb_o[0] = kc.astype(BF16)
    vcb_o[0] = vc.astype(BF16)
    gc_o[0] = proj[:, O_GC:O_QD]
    qd_o[0] = proj[:, O_QD:O_KD] * (DK_D ** -0.5)
    kd_o[0] = proj[:, O_KD:O_VD]
    vd_o[0] = proj[:, O_VD:O_GD]
    gd_o[0] = proj[:, O_GD:O_AD]
    pre = _mm_f32(proj[:, O_AD:N_PROJ_O_STAGED], aw_ref[...]) + ab_ref[...]
    la_o[0] = -_softplus(-pre) * (1.0 / GLA_TAU)


def _odd_prep(x, w_bf, alpha_w_pad, alpha_b):
    bsz, t, _ = x.shape
    tb = min(PROJ_BLOCK, t)
    nt = t // tb
    tok = lambda n: pl.BlockSpec((1, tb, n), lambda b, i: (b, i, 0))
    full = lambda shape: pl.BlockSpec(shape, lambda b, i: (0,) * len(shape))
    hk = H_D * DK_D
    widths = [(D_C, BF16), (D_C, F32), (D_C, F32), (D_C, BF16), (D_C, BF16), (D_C, F32),
              (hk, F32), (hk, F32), (D_D, F32), (hk, F32), (D_D, F32)]
    return pl.pallas_call(
        _odd_prep_body,
        grid=(bsz, nt),
        in_specs=[tok(D_MODEL), full((D_MODEL, N_PROJ_O_STAGED)), full((LANES, hk)), full((1, hk))],
        out_specs=[tok(n) for n, _ in widths],
        out_shape=[jax.ShapeDtypeStruct((bsz, t, n), dt) for n, dt in widths],
        compiler_params=_cparams("arbitrary", "arbitrary"),
        name="odd_prep",
    )(x, w_bf, alpha_w_pad, alpha_b.reshape(1, -1))


def _sb_scores(z):
    l = _log1pexp_negabs(z)
    return jnp.maximum(z, 0.0) + l, jnp.minimum(z, 0.0) - l


def _sb_prompt_body(q_ref, k_ref, v_ref, g_ref, bias_ref, us_ref, o_ref):
    i = pl.program_id(2)
    q = q_ref[0]
    tq = q.shape[0]
    lane = lax.broadcasted_iota(jnp.int32, q.shape, 1)
    first_head = lane < HD_C
    zero = jnp.zeros_like(q)
    qs = (jnp.where(first_head, q, zero), jnp.where(first_head, zero, q))
    bias = bias_ref[0]
    biases = (bias[:, 0:1], bias[:, HD_C:HD_C + 1])
    us = us_ref[...]
    row = lax.broadcasted_iota(jnp.int32, (tq, tq), 0)
    col = lax.broadcasted_iota(jnp.int32, (tq, tq), 1)
    causal = col < row

    def tile(j, carry, masked):
        start = pl.multiple_of(j * tq, tq)
        kblk = k_ref[0, pl.ds(start, tq), :]
        vblk = v_ref[0, pl.ds(start, tq), :]
        new = []
        for hh in range(2):
            acc, run = carry[hh]
            z = lax.dot_general(qs[hh], kblk, (((1,), (1,)), ((), ())), preferred_element_type=F32) + biases[hh]
            sp, lb = _sb_scores(z)
            if masked:
                sp = jnp.where(causal, sp, 0.0)
            later = _mm_sel(sp, us, 2)
            e = jnp.exp(lb - later - run)
            if masked:
                e = jnp.where(causal, e, 0.0)
            acc = acc + jnp.dot(e.astype(BF16), vblk, preferred_element_type=F32)
            run = run + later[:, 0:1] + sp[:, 0:1]
            new.append((acc, run))
        return tuple(new)

    init = tuple((jnp.zeros((tq, LANES), F32), jnp.zeros((tq, 1), F32)) for _ in range(2))
    carry = tile(i, init, True)
    carry = lax.fori_loop(0, i, lambda jj, c: tile(i - 1 - jj, c, False), carry)
    o = jnp.where(first_head, carry[0][0], carry[1][0])
    o_ref[0] = o * _silu(g_ref[0])


def _sb_prompt(q_bf, k_bf, v_bf, g_c, sb_bias):
    bsz, t, _ = q_bf.shape
    tq = min(SB_TILE, t)
    nq = t // tq
    pairs = D_C // LANES
    bias_lanes = jnp.repeat(sb_bias.astype(F32), HD_C).reshape(pairs, 1, LANES)
    us = jnp.asarray(np.tril(np.ones((tq, tq), np.float32), -1), BF16)
    qspec = pl.BlockSpec((1, tq, LANES), lambda b, p, i: (b, i, p))
    kvspec = pl.BlockSpec((1, t, LANES), lambda b, p, i: (b, 0, p))
    return pl.pallas_call(
        _sb_prompt_body,
        grid=(bsz, pairs, nq),
        in_specs=[qspec, kvspec, kvspec, qspec,
                  pl.BlockSpec((1, 1, LANES), lambda b, p, i: (p, 0, 0)),
                  pl.BlockSpec((tq, tq), lambda b, p, i: (0, 0))],
        out_specs=qspec,
        out_shape=jax.ShapeDtypeStruct((bsz, t, D_C), F32),
        compiler_params=_cparams("arbitrary", "arbitrary", "arbitrary"),
        name="sb_prompt",
    )(q_bf, k_bf, v_bf, g_c, bias_lanes, us)


def _sb_paged_body(pt_ref, qbd_ref, g_ref, bias_ref, ls_ref, ex_ref, *refs):
    npg = PAGES_PER_STEP
    k_refs = refs[:npg]
    v_refs = refs[npg:2 * npg]
    o_ref = refs[2 * npg]
    acc, run = refs[2 * npg + 1:]
    j = pl.program_id(1)

    @pl.when(j == 0)
    def _():
        acc[...] = jnp.zeros_like(acc)
        run[...] = jnp.zeros_like(run)

    qbd = qbd_ref[0]
    bias = bias_ref[...]
    ls = ls_ref[...]
    ex = ex_ref[...]
    for p in range(npg):
        kp = k_refs[p][0].astype(BF16)
        z = jnp.dot(kp, qbd, preferred_element_type=F32) + bias
        sp, lb = _sb_scores(z)
        later = _sel_mm(ls, sp, 2)
        e = jnp.exp(lb - later - run[...])
        w = jnp.dot(e.astype(BF16), ex, preferred_element_type=F32)
        wv = w * v_refs[p][0]
        acc[...] += jnp.sum(wv.reshape(PAGE // 8, 8, D_C), axis=0)
        run[...] += later[0:1, :] + sp[0:1, :]

    @pl.when(j == pl.num_programs(1) - 1)
    def _():
        o = jnp.sum(acc[...], axis=0, keepdims=True)
        o_ref[0] = o * _silu(g_ref[0])


def _sb_paged(q_s, g_s, cache_k, cache_v, page_table, sb_bias):
    bsz, n_pages = page_table.shape
    npg = PAGES_PER_STEP
    steps = n_pages // npg
    head_of = np.arange(D_C) // HD_C
    sel = jnp.asarray(head_of[:, None] == np.arange(LANES)[None, :], F32)
    qbd = (q_s[:, :, None] * sel[None]).astype(BF16)
    bias = jnp.zeros((1, LANES), F32).at[0, :H_C].set(sb_bias.astype(F32))
    ls = jnp.asarray(np.triu(np.ones((PAGE, PAGE), np.float32), 1), BF16)
    ex = jnp.asarray(np.arange(LANES)[:, None] == head_of[None, :], BF16)

    def page_spec(p):
        return pl.BlockSpec((1, PAGE, D_C), lambda b, j, pt: (pt[b, n_pages - 1 - (j * npg + p)], 0, 0))

    row = pl.BlockSpec((1, 1, D_C), lambda b, j, pt: (b, 0, 0))
    grid_spec = pltpu.PrefetchScalarGridSpec(
        num_scalar_prefetch=1,
        grid=(bsz, steps),
        in_specs=[pl.BlockSpec((1, D_C, LANES), lambda b, j, pt: (b, 0, 0)), row,
                  pl.BlockSpec((1, LANES), lambda b, j, pt: (0, 0)),
                  pl.BlockSpec((PAGE, PAGE), lambda b, j, pt: (0, 0)),
                  pl.BlockSpec((LANES, D_C), lambda b, j, pt: (0, 0))]
                 + [page_spec(p) for p in range(npg)] * 2,
        out_specs=row,
        scratch_shapes=[pltpu.VMEM((8, D_C), F32), pltpu.VMEM((1, LANES), F32)],
    )
    out = pl.pallas_call(
        _sb_paged_body,
        grid_spec=grid_spec,
        out_shape=jax.ShapeDtypeStruct((bsz, 1, D_C), F32),
        compiler_params=_cparams("arbitrary", "arbitrary"),
        name="sb_paged",
    )(page_table, qbd, g_s.reshape(bsz, 1, D_C), bias, ls, ex, *([cache_k] * npg), *([cache_v] * npg))
    return out


def _gla_body(q_ref, k_ref, v_ref, la_ref, gd_ref, s0_ref, dmat_ref, gng_ref, o_ref, sf_o, state):
    t_idx = pl.program_id(1)

    @pl.when(t_idx == 0)
    def _():
        state[...] = s0_ref[0]

    q = q_ref[0]
    k = k_ref[0]
    n = q.shape[0]
    dec = jnp.exp(_sel_mm(dmat_ref[...], la_ref[0], 3))
    blk = lambda idx: dec[n * idx:n * (idx + 1), :]
    q_in = q * blk(0)
    k_out = k * blk(1)
    e_last = blk(0)[n - 1:n, :]
    row = lax.broadcasted_iota(jnp.int32, (n, n), 0)
    col = lax.broadcasted_iota(jnp.int32, (n, n), 1)
    rsub = lax.broadcasted_iota(jnp.int32, q.shape, 0)
    levels = _gla_levels(n)
    qk_levels = []
    for li, m in enumerate(levels):
        upper = (rsub & (2 * m - 1)) >= m
        e = blk(2 + li)
        shift = int(math.log2(2 * m))
        qk_levels.append((jnp.where(upper, q * e, 0.0), jnp.where(upper, 0.0, k * e),
                          (row >> shift) == (col >> shift)))
    outs = []
    for h in range(H_D):
        sl = slice(DK_D * h, DK_D * (h + 1))
        vh = v_ref[0][:, DV_D * h:DV_D * (h + 1)]
        st = state[h]
        scores = jnp.where(row == col, _mm_nt(q[:, sl], k[:, sl]), 0.0)
        for qm, km, same in qk_levels:
            scores = scores + jnp.where(same, _mm_nt(qm[:, sl], km[:, sl]), 0.0)
        o = _mm_nt(q_in[:, sl], st) + _mm(scores, vh)
        state[h] = st * e_last[:, sl] + _mm_tn(vh, k_out[:, sl])
        outs.append(o * lax.rsqrt(jnp.mean(o * o, axis=-1, keepdims=True) + RMS_EPS))
    o_ref[0] = jnp.concatenate(outs, axis=-1) * gng_ref[...] * _silu(gd_ref[0])

    @pl.when(t_idx == pl.num_programs(1) - 1)
    def _():
        sf_o[0] = state[...]


def _gla_scan(q, k, v, la, gd, s0_t, gn_g, chunk):
    bsz, t, hk = q.shape
    nt = t // chunk
    tok = lambda n: pl.BlockSpec((1, chunk, n), lambda b, i: (b, i, 0))
    st = pl.BlockSpec((1, H_D, DV_D, DK_D), lambda b, i: (b, 0, 0, 0))
    full = lambda shape: pl.BlockSpec(shape, lambda b, i: (0,) * len(shape))
    dmat = jnp.asarray(_gla_decay_rows(chunk), BF16)
    return pl.pallas_call(
        _gla_body,
        grid=(bsz, nt),
        in_specs=[tok(hk), tok(hk), tok(D_D), tok(hk), tok(D_D), st, full(dmat.shape), full((1, D_D))],
        out_specs=[tok(D_D), st],
        out_shape=[jax.ShapeDtypeStruct((bsz, t, D_D), F32),
                   jax.ShapeDtypeStruct((bsz, H_D, DV_D, DK_D), F32)],
        scratch_shapes=[pltpu.VMEM((H_D, DV_D, DK_D), F32)],
        compiler_params=_cparams("arbitrary", "arbitrary"),
        name="gla_scan_c%d" % chunk,
    )(q, k, v, la, gd, s0_t, dmat, gn_g.reshape(1, -1))


def _pad_tokens(a, n):
    return jnp.pad(a, ((0, 0), (0, n - a.shape[1]), (0, 0)))


def kernel(x_prompt, x_sample, state_rwkv, state_shift, state_s5_re, state_s5_im, cache_k, cache_v, state_gla,
           page_table, ln_g, ln_b, w_in_e, mu_shift, rwkv_w0, rwkv_w2, rwkv_a0, rwkv_a2, rwkv_k_k, rwkv_k_a,
           rwkv_r_k, rwkv_gn_g, rwkv_gn_b, s5_lambda_re, s5_lambda_im, s5_log_dt, s5_b_re, s5_b_im, s5_c_re,
           s5_c_im, s5_d, s5_glu_w, s5_glu_b, w_out_e, w_in_o, sb_bias, gla_alpha_w, gla_alpha_b, gla_gn_g,
           w_out_o):
    bp, t_p, _ = x_prompt.shape
    bs = x_sample.shape[0]
    n_phys = cache_k.shape[1]
    xs_rows = x_sample.reshape(1, bs, D_MODEL)

    w_in_bf = w_in_e[0].astype(BF16)
    prep_w = (w_in_bf, mu_shift[0], rwkv_w0[0], rwkv_w2[0], rwkv_a0[0], rwkv_a2[0], rwkv_k_k[0], rwkv_k_a[0])
    pe = _even_prep(x_prompt, jnp.zeros((bp, 1, N_SHIFT), F32), *prep_w, decode=False)
    se = _even_prep(xs_rows, state_shift[0].reshape(1, bs, N_SHIFT), *prep_w, decode=True)
    rw_w = (rwkv_gn_g[0], rwkv_gn_b[0], rwkv_r_k[0])
    oa_p, rw_p = _rwkv_scan(*pe[:7], jnp.zeros((bp, H_A, HD_A, HD_A), F32), *rw_w, chunk=RWKV_CHUNK)
    se_tok = [_pad_tokens(a.reshape(bs, 1, D_A), DECODE_CHUNK) for a in se[:7]]
    oa_s, rw_s = _rwkv_scan(*se_tok, state_rwkv[0], *rw_w, chunk=DECODE_CHUNK)

    abar_re, abar_im, bbr, bbi, cr, ci = _s5_params(s5_lambda_re[0], s5_lambda_im[0], s5_log_dt[0], s5_b_re[0],
                                                    s5_b_im[0], s5_c_re[0], s5_c_im[0])
    s5_tail = (cr.astype(BF16), ci.astype(BF16), s5_d[0], s5_glu_w[0].astype(BF16), s5_glu_b[0])
    zst = jnp.zeros((bp, 1, N_STATE_B), F32)
    ob_p, re_p, im_p = _s5_mixer(pe[7], pe[8], zst, zst, abar_re, abar_im, bbr.astype(BF16), bbi.astype(BF16),
                                 *s5_tail, decode=False)
    ob_s, re_s, im_s = _s5_mixer(se[7], se[8], state_s5_re[0].reshape(1, bs, N_STATE_B),
                                 state_s5_im[0].reshape(1, bs, N_STATE_B), abar_re, abar_im, bbr, bbi,
                                 *s5_tail, decode=True)

    w_out_e_bf = w_out_e[0].astype(BF16)
    x1_p = _out_norm(oa_p, ob_p, x_prompt, w_out_e_bf, ln_g[0], ln_b[0])
    x1_s = _out_norm(oa_s[:, 0:1, :].reshape(1, bs, D_A), ob_s, xs_rows, w_out_e_bf, ln_g[0], ln_b[0])

    w = w_in_o[0]
    w_o_bf = jnp.concatenate([w[:, :O_GD], w[:, O_GD + R_G:], w[:, O_GD:O_GD + R_G],
                              jnp.zeros((D_MODEL, N_PROJ_O_STAGED - w.shape[1]), F32)], axis=1).astype(BF16)
    alpha_w_pad = jnp.pad(gla_alpha_w[0], ((0, LANES - R_G), (0, 0)))
    po = _odd_prep(x1_p, w_o_bf, alpha_w_pad, gla_alpha_b[0])
    so = _odd_prep(x1_s, w_o_bf, alpha_w_pad, gla_alpha_b[0])
    qc_p, kc_p, vc_p, kcb_p, vcb_p, gc_p, qd_p, kd_p, vd_p, la_p, gd_p = po
    qc_s, kc_s, vc_s, _, _, gc_s, qd_s, kd_s, vd_s, la_s, gd_s = so

    oc_p = _sb_prompt(qc_p, kcb_p, vcb_p, gc_p, sb_bias[0])
    oc_s = _sb_paged(qc_s.reshape(bs, D_C).astype(F32), gc_s.reshape(bs, D_C),
                     cache_k[0].reshape(n_phys, PAGE, D_C), cache_v[0].reshape(n_phys, PAGE, D_C),
                     page_table, sb_bias[0])

    od_p, gl_p = _gla_scan(qd_p, kd_p, vd_p, la_p, gd_p, jnp.zeros((bp, H_D, DV_D, DK_D), F32), gla_gn_g[0],
                           chunk=GLA_CHUNK)
    tok_s = [_pad_tokens(a.reshape(bs, 1, a.shape[-1]), DECODE_CHUNK) for a in (qd_s, kd_s, vd_s, la_s, gd_s)]
    od_s, gl_s = _gla_scan(*tok_s, jnp.swapaxes(state_gla[0], -1, -2), gla_gn_g[0], chunk=DECODE_CHUNK)

    w_out_o_bf = w_out_o[0].astype(BF16)
    y_p = _out_norm(oc_p, od_p, x1_p, w_out_o_bf, ln_g[1], ln_b[1])
    y_s = _out_norm(oc_s.reshape(1, bs, D_C), od_s[:, 0:1, :].reshape(1, bs, D_D), x1_s, w_out_o_bf, ln_g[1],
                    ln_b[1])

    heads_c = lambda a, b, t: a.reshape(1, b, t, H_C, HD_C)
    return (y_p, y_s.reshape(bs, 1, D_MODEL),
            rw_p[None], rw_s[None],
            pe[9].reshape(1, bp, N_SHIFT), se[9].reshape(1, bs, N_SHIFT),
            re_p.reshape(1, bp, G_B, P_B), re_s.reshape(1, bs, G_B, P_B),
            im_p.reshape(1, bp, G_B, P_B), im_s.reshape(1, bs, G_B, P_B),
            heads_c(kc_p, bp, t_p), heads_c(kc_s, bs, 1), heads_c(vc_p, bp, t_p), heads_c(vc_s, bs, 1),
            jnp.swapaxes(gl_p, -1, -2)[None], jnp.swapaxes(gl_s, -1, -2)[None])
```

```python
import functools
import math

import numpy as np
import jax
import jax.numpy as jnp
from jax import lax
from jax.experimental import pallas as pl
from jax.experimental.pallas import tpu as pltpu

F32 = jnp.float32
BF16 = jnp.bfloat16

D_MODEL = 1024
D_A = 512
HD_A = 64
H_A = 8
R_W = 64
R_A = 64
D_B = 512
S5_GROUP = 16
G_B = 32
P_B = 64
N_STATE_B = G_B * P_B
D_C = 512
HD_C = 64
H_C = 8
D_D = 512
H_D = 4
DV_D = 128
DK_D = 64
R_G = 16
GLA_TAU = 16.0
N_SHIFT = 3 * D_A + R_W + R_A
N_PROJ_E = N_SHIFT + D_A + 2 * D_B
DEPTH = 2
DEEPNORM_ALPHA = (2.0 * DEPTH) ** 0.25
LN_EPS = 1e-5
GN_EPS = 64e-5
RMS_EPS = 1e-5
PAGE = 128

LOG2E = 1.4426950408889634
LANES = 128
VMEM_LIMIT = 56 * 1024 * 1024

PROJ_BLOCK = 512
S5_BLOCK = 256
RWKV_CHUNK = 64
GLA_CHUNK = 64
DECODE_CHUNK = 8
SB_TILE = 256
SB_HEADS = 4
PAGES_PER_STEP = 8


def _cparams(*sem):
    return pltpu.CompilerParams(dimension_semantics=sem, vmem_limit_bytes=VMEM_LIMIT)


def _mm(a, b):
    return jnp.dot(a.astype(BF16), b.astype(BF16), preferred_element_type=F32)


def _mm_nt(a, b):
    return lax.dot_general(a.astype(BF16), b.astype(BF16), (((1,), (1,)), ((), ())),
                           preferred_element_type=F32)


def _mm_tn(a, b):
    return lax.dot_general(a.astype(BF16), b.astype(BF16), (((0,), (0,)), ((), ())),
                           preferred_element_type=F32)


def _mm_f32(a, b):
    return jnp.dot(a, b, precision=lax.Precision.HIGHEST, preferred_element_type=F32)


def _mmw(a, w):
    if w.dtype == BF16:
        return jnp.dot(a.astype(BF16), w, preferred_element_type=F32)
    return _mm_f32(a, w)


def _split(x, parts):
    out = []
    rem = x
    for _ in range(parts - 1):
        p = rem.astype(BF16)
        out.append(p)
        rem = rem - p.astype(F32)
    out.append(rem.astype(BF16))
    return out


def _sel_mm(sel, x, parts):
    acc = None
    for p in _split(x, parts):
        t = jnp.dot(sel, p, preferred_element_type=F32)
        acc = t if acc is None else acc + t
    return acc


def _mm_sel(x, sel, parts):
    acc = None
    for p in _split(x, parts):
        t = jnp.dot(p, sel, preferred_element_type=F32)
        acc = t if acc is None else acc + t
    return acc


def _sigmoid(t):
    return 1.0 / (1.0 + jnp.exp(-t))


def _silu(t):
    return t * _sigmoid(t)


def _log1pexp_negabs(t):
    return jnp.log(1.0 + jnp.exp2(jnp.abs(t) * (-LOG2E)))


def _softplus(t):
    return jnp.maximum(t, 0.0) + _log1pexp_negabs(t)


def _gelu_tanh(x):
    return 0.5 * x * (1.0 + jnp.tanh(math.sqrt(2.0 / math.pi) * (x + 0.044715 * (x * x * x))))


def _tri_incl(n):
    return np.tril(np.ones((n, n), np.float32))


def _block_ones(n, blk):
    idx = np.arange(n) // blk
    return (idx[:, None] == idx[None, :]).astype(np.float32)


def _gla_decay_rows(n):
    t = np.arange(n)[:, None]
    i = np.arange(n)[None, :]
    mats = [(i <= t), (i > t)]
    m = n // 2
    while m >= 1:
        p = 2 * m * (t // (2 * m)) + m - 1
        upper = (t % (2 * m)) >= m
        mats.append(np.where(upper, (i > p) & (i <= t), (i > t) & (i <= p)))
        m //= 2
    return np.concatenate([x.astype(np.float32) for x in mats], axis=0)


def _gla_levels(n):
    out = []
    m = n // 2
    while m >= 1:
        out.append(m)
        m //= 2
    return out


def _even_prep_body(decode, x_ref, zp_ref, w_ref, mu_ref, w0_ref, w2_ref, a0_ref, a2_ref, kk_ref, ka_ref,
                    ones_ref, r_o, lw_o, k_o, v_o, aa_o, bb_o, ga_o, u_o, gb_o, sh_o, carry):
    x = x_ref[0].astype(BF16)
    proj = jnp.dot(x, w_ref[...], preferred_element_type=F32)
    z = proj[:, :N_SHIFT]
    tb = z.shape[0]
    if decode:
        zp = zp_ref[0]
        sh_o[0] = z
    else:
        @pl.when(pl.program_id(1) == 0)
        def _():
            carry[...] = zp_ref[0]
        row = lax.broadcasted_iota(jnp.int32, z.shape, 0)
        zp = jnp.where(row == 0, carry[...], pltpu.roll(z, 1, 0))
        carry[...] = z[tb - 1:tb, :]
        sh_o[0] = z[tb - 1:tb, :]
    zm = z + mu_ref[...] * (zp - z)
    r = zm[:, 0:D_A]
    k = zm[:, D_A:2 * D_A]
    v = zm[:, 2 * D_A:3 * D_A]
    wd = zm[:, 3 * D_A:3 * D_A + R_W]
    ad = zm[:, 3 * D_A + R_W:N_SHIFT]
    w = -_softplus(-(w0_ref[...] + _mm_f32(jnp.tanh(wd), w2_ref[...]))) - 0.5
    a = _sigmoid(a0_ref[...] + _mm_f32(ad, a2_ref[...]))
    kk = k * kk_ref[...]
    ss = _mm_sel(kk * kk, ones_ref[...], 2)
    kkn = kk / jnp.maximum(jnp.sqrt(ss), 1e-12)
    r_o[0] = r
    lw_o[0] = -jnp.exp(w)
    k_o[0] = k * (1.0 + (a - 1.0) * ka_ref[...])
    v_o[0] = v
    aa_o[0] = -kkn
    bb_o[0] = kkn * a
    ga_o[0] = proj[:, N_SHIFT:N_SHIFT + D_A]
    u_o[0] = proj[:, N_SHIFT + D_A:N_SHIFT + D_A + D_B]
    gb_o[0] = proj[:, N_SHIFT + D_A + D_B:N_PROJ_E]


def _even_prep(x, zprev, w_bf, mu, w0, w2, a0, a2, k_k, k_a, decode):
    bsz, t, _ = x.shape
    tb = t if decode else min(PROJ_BLOCK, t)
    nt = t // tb
    row2 = lambda a: a.reshape(1, -1)
    ones = jnp.asarray(_block_ones(D_A, HD_A), BF16)
    full = lambda shape: pl.BlockSpec(shape, lambda b, i: (0,) * len(shape))
    tok = lambda n: pl.BlockSpec((1, tb, n), lambda b, i: (b, i, 0))
    zp_spec = (pl.BlockSpec((1, tb, N_SHIFT), lambda b, i: (b, i, 0)) if decode
               else pl.BlockSpec((1, 1, N_SHIFT), lambda b, i: (b, 0, 0)))
    sh_rows = tb if decode else 1
    outs = [jax.ShapeDtypeStruct((bsz, t, D_A), F32)] * 9 + [jax.ShapeDtypeStruct((bsz, sh_rows, N_SHIFT), F32)]
    out_specs = [tok(D_A)] * 9 + [pl.BlockSpec((1, sh_rows, N_SHIFT), lambda b, i: (b, 0, 0))]
    return pl.pallas_call(
        functools.partial(_even_prep_body, decode),
        grid=(bsz, nt),
        in_specs=[tok(D_MODEL), zp_spec, full((D_MODEL, N_PROJ_E)), full((1, N_SHIFT)), full((1, D_A)),
                  full((R_W, D_A)), full((1, D_A)), full((R_A, D_A)), full((1, D_A)), full((1, D_A)),
                  full((D_A, D_A))],
        out_specs=out_specs,
        out_shape=outs,
        scratch_shapes=[pltpu.VMEM((1, N_SHIFT), F32)],
        compiler_params=_cparams("arbitrary", "arbitrary"),
        name="even_prep_decode" if decode else "even_prep",
    )(x, zprev, w_bf, row2(mu), row2(w0), w2, row2(a0), a2, row2(k_k), row2(k_a), ones)


def _rwkv_body(r_ref, lw_ref, k_ref, v_ref, aa_ref, bb_ref, ga_ref, s0_ref, tri_ref, ones_ref, gng_ref, gnb_ref,
               rk_ref, oa_o, sf_o, state):
    t_idx = pl.program_id(1)

    @pl.when(t_idx == 0)
    def _():
        state[...] = s0_ref[0]

    lw = lw_ref[0]
    n = lw.shape[0]
    c = _sel_mm(tri_ref[...], lw, 3)
    c_last = c[n - 1:n, :]
    r = r_ref[0]
    k = k_ref[0]
    v = v_ref[0]
    bb = bb_ref[0]
    pinv = jnp.exp(-c)
    a_t = aa_ref[0] * jnp.exp(c - lw)
    b_t = bb * pinv
    k_t = k * pinv
    r_t = r * jnp.exp(c)
    tail = jnp.exp(c_last - c)
    b_l = bb * tail
    k_l = k * tail
    p_last = jnp.exp(c_last)
    rkr = r * k * rk_ref[...]

    row = lax.broadcasted_iota(jnp.int32, (n, n), 0)
    col = lax.broadcasted_iota(jnp.int32, (n, n), 1)
    strict = col < row
    incl = col <= row
    levels = int(math.log2(n))

    heads = range(H_A)
    sls = [slice(HD_A * h, HD_A * (h + 1)) for h in heads]
    cast = lambda x: [x[:, sl].astype(BF16) for sl in sls]
    a_h, b_h, k_h, r_h, v_h = cast(a_t), cast(b_t), cast(k_t), cast(r_t), cast(v)
    s_h = [state[h] for h in heads]
    s_bf = [s.astype(BF16) for s in s_h]
    nmat = [jnp.where(strict, _mm_nt(a_h[h], b_h[h]), 0.0) for h in heads]
    mmat = [jnp.where(strict, _mm_nt(a_h[h], k_h[h]), 0.0) for h in heads]
    rbm = [jnp.where(incl, _mm_nt(r_h[h], b_h[h]), 0.0) for h in heads]
    rkm = [jnp.where(incl, _mm_nt(r_h[h], k_h[h]), 0.0) for h in heads]
    u = [_mm_nt(a_h[h], s_bf[h]) + _mm(mmat[h], v_h[h]) for h in heads]
    y0 = [_mm_nt(r_h[h], s_bf[h]) + _mm(rkm[h], v_h[h]) for h in heads]
    npow = nmat
    for lvl in range(levels):
        u = [u[h] + _mm(npow[h], u[h]) for h in heads]
        if lvl + 1 < levels:
            npow = [_mm(npow[h], npow[h]) for h in heads]
    y = [y0[h] + _mm(rbm[h], u[h]) for h in heads]
    b_lh, k_lh = cast(b_l), cast(k_l)
    new_s = [s_h[h] * p_last[:, sls[h]] + _mm_tn(u[h], b_lh[h]) + _mm_tn(v_h[h], k_lh[h]) for h in heads]
    for h in heads:
        state[h] = new_s[h]
    yn = []
    for h in heads:
        mu = jnp.mean(y[h], axis=-1, keepdims=True)
        var = jnp.mean(jnp.square(y[h] - mu), axis=-1, keepdims=True)
        yn.append((y[h] - mu) * lax.rsqrt(var + GN_EPS))
    yn = jnp.concatenate(yn, axis=-1)
    rk_head = _mm_sel(rkr, ones_ref[...], 2)
    oa_o[0] = (yn * gng_ref[...] + gnb_ref[...] + rk_head * v) * _silu(ga_ref[0])

    @pl.when(t_idx == pl.num_programs(1) - 1)
    def _():
        sf_o[0] = state[...]


def _rwkv_scan(r, lw, k, v, aa, bb, ga, s0, gn_g, gn_b, r_k, chunk):
    bsz, t, _ = r.shape
    nt = t // chunk
    tok = pl.BlockSpec((1, chunk, D_A), lambda b, i: (b, i, 0))
    st = pl.BlockSpec((1, H_A, HD_A, HD_A), lambda b, i: (b, 0, 0, 0))
    full = lambda shape: pl.BlockSpec(shape, lambda b, i: (0,) * len(shape))
    tri = jnp.asarray(_tri_incl(chunk), BF16)
    ones = jnp.asarray(_block_ones(D_A, HD_A), BF16)
    return pl.pallas_call(
        _rwkv_body,
        grid=(bsz, nt),
        in_specs=[tok] * 7 + [st, full((chunk, chunk)), full((D_A, D_A)), full((1, D_A)), full((1, D_A)),
                  full((1, D_A))],
        out_specs=[tok, st],
        out_shape=[jax.ShapeDtypeStruct((bsz, t, D_A), F32),
                   jax.ShapeDtypeStruct((bsz, H_A, HD_A, HD_A), F32)],
        scratch_shapes=[pltpu.VMEM((H_A, HD_A, HD_A), F32)],
        compiler_params=_cparams("arbitrary", "arbitrary"),
        name="rwkv_scan_c%d" % chunk,
    )(r, lw, k, v, aa, bb, ga, s0, tri, ones, gn_g.reshape(1, -1), gn_b.reshape(1, -1), r_k.reshape(1, -1))


S5_LANE_BLOCKS = D_B // LANES
S5_STATES_PER_BLOCK = N_STATE_B // S5_LANE_BLOCKS


def _s5_body(decode, u_ref, gb_ref, h0r_ref, h0i_ref, ar_ref, ai_ref, bbr_ref, bbi_ref, cr_ref, ci_ref, d_ref,
             gw_ref, gbias_ref, ob_o, hr_o, hi_o, car_r, car_i):
    u = u_ref[0]
    tb = u.shape[0]
    if not decode:
        @pl.when(pl.program_id(1) == 0)
        def _():
            car_r[...] = h0r_ref[0]
            car_i[...] = h0i_ref[0]
        row = lax.broadcasted_iota(jnp.int32, (tb, S5_STATES_PER_BLOCK), 0)
    ys = []
    for j in range(S5_LANE_BLOCKS):
        sl = slice(S5_STATES_PER_BLOCK * j, S5_STATES_PER_BLOCK * (j + 1))
        uj = u[:, LANES * j:LANES * (j + 1)]
        bur = _mmw(uj, bbr_ref[j])
        bui = _mmw(uj, bbi_ref[j])
        ar = ar_ref[:, sl]
        ai = ai_ref[:, sl]
        if decode:
            h0r = h0r_ref[0][:, sl]
            h0i = h0i_ref[0][:, sl]
            hr = ar * h0r - ai * h0i + bur
            hi = ar * h0i + ai * h0r + bui
            hr_o[0, :, sl] = hr
            hi_o[0, :, sl] = hi
        else:
            cr = car_r[:, sl]
            ci = car_i[:, sl]
            first = row == 0
            hr = bur + jnp.where(first, ar * cr - ai * ci, 0.0)
            hi = bui + jnp.where(first, ar * ci + ai * cr, 0.0)
            pr, pi = ar, ai
            s = 1
            while s < tb:
                keep = row >= s
                sr = jnp.where(keep, pltpu.roll(hr, s, 0), 0.0)
                si = jnp.where(keep, pltpu.roll(hi, s, 0), 0.0)
                hr, hi = hr + pr * sr - pi * si, hi + pr * si + pi * sr
                pr, pi = pr * pr - pi * pi, 2.0 * pr * pi
                s *= 2
            car_r[:, sl] = hr[tb - 1:tb, :]
            car_i[:, sl] = hi[tb - 1:tb, :]
            hr_o[0, :, sl] = hr[tb - 1:tb, :]
            hi_o[0, :, sl] = hi[tb - 1:tb, :]
        ys.append(_mm(hr, cr_ref[j]) - _mm(hi, ci_ref[j]))
    y = jnp.concatenate(ys, axis=-1) + d_ref[...] * u
    y = _gelu_tanh(y)
    y = y * _sigmoid(_mm(y, gw_ref[...]) + gbias_ref[...])
    ob_o[0] = y * _silu(gb_ref[0])


def _s5_mixer(u, gate_b, h0r, h0i, abar_re, abar_im, bbr, bbi, cr, ci, d_skip, glu_w, glu_b, decode):
    bsz, t, _ = u.shape
    tb = t if decode else min(S5_BLOCK, t)
    nt = t // tb
    st_rows = tb if decode else 1
    tok = pl.BlockSpec((1, tb, D_B), lambda b, i: (b, i, 0))
    st = pl.BlockSpec((1, st_rows, N_STATE_B), lambda b, i: (b, 0, 0))
    full = lambda shape: pl.BlockSpec(shape, lambda b, i: (0,) * len(shape))
    return pl.pallas_call(
        functools.partial(_s5_body, decode),
        grid=(bsz, nt),
        in_specs=[tok, tok, st, st, full((1, N_STATE_B)), full((1, N_STATE_B)),
                  full(bbr.shape), full(bbi.shape), full(cr.shape), full(ci.shape),
                  full((1, D_B)), full((D_B, D_B)), full((1, D_B))],
        out_specs=[tok, st, st],
        out_shape=[jax.ShapeDtypeStruct((bsz, t, D_B), F32),
                   jax.ShapeDtypeStruct((bsz, st_rows, N_STATE_B), F32),
                   jax.ShapeDtypeStruct((bsz, st_rows, N_STATE_B), F32)],
        scratch_shapes=[pltpu.VMEM((1, N_STATE_B), F32), pltpu.VMEM((1, N_STATE_B), F32)],
        compiler_params=_cparams("arbitrary", "arbitrary"),
        name="s5_decode" if decode else "s5_scan",
    )(u, gate_b, h0r, h0i, abar_re, abar_im, bbr, bbi, cr, ci, d_skip.reshape(1, -1), glu_w,
      glu_b.reshape(1, -1))


def _s5_params(lam_re, lam_im, log_dt, b_re, b_im, c_re, c_im):
    dt = jnp.exp(log_dt)[:, None]
    mag = jnp.exp(lam_re * dt)
    abar_re, abar_im = mag * jnp.cos(lam_im * dt), mag * jnp.sin(lam_im * dt)
    den = lam_re * lam_re + lam_im * lam_im
    nr, ni = abar_re - 1.0, abar_im
    f_re, f_im = (nr * lam_re + ni * lam_im) / den, (ni * lam_re - nr * lam_im) / den
    bb_re = f_re[..., None] * b_re - f_im[..., None] * b_im
    bb_im = f_re[..., None] * b_im + f_im[..., None] * b_re
    gpb = G_B // S5_LANE_BLOCKS
    eye = jnp.eye(gpb, dtype=F32)

    def stage_b(m):
        m = m.reshape(S5_LANE_BLOCKS, gpb, P_B, S5_GROUP)
        blk = jnp.einsum('jgpc,gh->jgchp', m, eye)
        return blk.reshape(S5_LANE_BLOCKS, gpb * S5_GROUP, gpb * P_B)

    def stage_c(m):
        m = m.reshape(S5_LANE_BLOCKS, gpb, S5_GROUP, P_B)
        blk = jnp.einsum('jgcp,gh->jgphc', m, eye)
        return blk.reshape(S5_LANE_BLOCKS, gpb * P_B, gpb * S5_GROUP)

    return (abar_re.reshape(1, -1), abar_im.reshape(1, -1), stage_b(bb_re), stage_b(bb_im),
            stage_c(c_re), stage_c(c_im))


def _out_norm_body(a_ref, b_ref, x_ref, w_ref, g_ref, beta_ref, o_ref):
    half = a_ref.shape[-1]
    out = _mmw(a_ref[0], w_ref[0:half, :]) + _mmw(b_ref[0], w_ref[half:2 * half, :])
    h = DEEPNORM_ALPHA * x_ref[0] + out
    mu = jnp.mean(h, axis=-1, keepdims=True)
    var = jnp.mean(jnp.square(h - mu), axis=-1, keepdims=True)
    o_ref[0] = (h - mu) * lax.rsqrt(var + LN_EPS) * g_ref[...] + beta_ref[...]


def _out_norm(a, b, x, w_bf, g, beta):
    bsz, t, half = a.shape
    tb = min(PROJ_BLOCK, t)
    nt = t // tb
    tok = lambda n: pl.BlockSpec((1, tb, n), lambda bi, i: (bi, i, 0))
    full = lambda shape: pl.BlockSpec(shape, lambda bi, i: (0,) * len(shape))
    return pl.pallas_call(
        _out_norm_body,
        grid=(bsz, nt),
        in_specs=[tok(half), tok(half), tok(D_MODEL), full((2 * half, D_MODEL)), full((1, D_MODEL)),
                  full((1, D_MODEL))],
        out_specs=tok(D_MODEL),
        out_shape=jax.ShapeDtypeStruct((bsz, t, D_MODEL), F32),
        compiler_params=_cparams("arbitrary", "arbitrary"),
        name="out_norm",
    )(a, b, x, w_bf, g.reshape(1, -1), beta.reshape(1, -1))


O_QC, O_KC, O_VC, O_GC = 0, 512, 1024, 1536
O_QD, O_KD, O_VD, O_GD, O_AD = 2048, 2304, 2560, 3072, 3584
N_PROJ_O_STAGED = 3712


def _odd_prep_body(x_ref, w_ref, aw_ref, ab_ref, qc_o, kc_o, vc_o, kcb_o, vcb_o, gc_o, qd_o, kd_o, vd_o, la_o,
                   gd_o):
    x = x_ref[0].astype(BF16)
    proj = jnp.dot(x, w_ref[...], preferred_element_type=F32)
    kc = proj[:, O_KC:O_VC]
    vc = proj[:, O_VC:O_GC]
    qc_o[0] = (proj[:, O_QC:O_KC] * (HD_C ** -0.5)).astype(BF16)
    kc_o[0] = kc
    vc_o[0] = vc
    kcb_o[0] = kc.astype(BF16)
    vcb_o[0] = vc.astype(BF16)
    gc_o[0] = proj[:, O_GC:O_QD]
    qd_o[0] = proj[:, O_QD:O_KD] * (DK_D ** -0.5)
    kd_o[0] = proj[:, O_KD:O_VD]
    vd_o[0] = proj[:, O_VD:O_GD]
    gd_o[0] = proj[:, O_GD:O_AD]
    pre = _mm_f32(proj[:, O_AD:N_PROJ_O_STAGED], aw_ref[...]) + ab_ref[...]
    la_o[0] = -_softplus(-pre) * (1.0 / GLA_TAU)


def _odd_prep(x, w_bf, alpha_w_pad, alpha_b):
    bsz, t, _ = x.shape
    tb = min(PROJ_BLOCK, t)
    nt = t // tb
    tok = lambda n: pl.BlockSpec((1, tb, n), lambda b, i: (b, i, 0))
    full = lambda shape: pl.BlockSpec(shape, lambda b, i: (0,) * len(shape))
    hk = H_D * DK_D
    widths = [(D_C, BF16), (D_C, F32), (D_C, F32), (D_C, BF16), (D_C, BF16), (D_C, F32),
              (hk, F32), (hk, F32), (D_D, F32), (hk, F32), (D_D, F32)]
    return pl.pallas_call(
        _odd_prep_body,
        grid=(bsz, nt),
        in_specs=[tok(D_MODEL), full((D_MODEL, N_PROJ_O_STAGED)), full((LANES, hk)), full((1, hk))],
        out_specs=[tok(n) for n, _ in widths],
        out_shape=[jax.ShapeDtypeStruct((bsz, t, n), dt) for n, dt in widths],
        compiler_params=_cparams("arbitrary", "arbitrary"),
        name="odd_prep",
    )(x, w_bf, alpha_w_pad, alpha_b.reshape(1, -1))


def _sb_prompt_body(q_ref, k_ref, v_ref, g_ref, bias_ref, ui_ref, o_ref):
    i = pl.program_id(2)
    q = q_ref[0]
    tq, width = q.shape
    heads = range(width // HD_C)
    lane = lax.broadcasted_iota(jnp.int32, q.shape, 1)
    zero = jnp.zeros_like(q)
    qs = [jnp.where((lane >= HD_C * h) & (lane < HD_C * (h + 1)), q, zero) for h in heads]
    bias = bias_ref[0]
    biases = [bias[:, HD_C * h:HD_C * h + 1] for h in heads]
    ui = ui_ref[...]
    row = lax.broadcasted_iota(jnp.int32, (tq, tq), 0)
    col = lax.broadcasted_iota(jnp.int32, (tq, tq), 1)
    causal = col < row

    def tile(j, carry, masked):
        start = pl.multiple_of(j * tq, tq)
        kblk = k_ref[0, pl.ds(start, tq), :]
        vblk = v_ref[0, pl.ds(start, tq), :]
        zs = [lax.dot_general(qs[h], kblk, (((1,), (1,)), ((), ())), preferred_element_type=F32) + biases[h]
              for h in heads]
        incls = []
        for z in zs:
            sp = _softplus(z)
            if masked:
                sp = jnp.where(causal, sp, 0.0)
            incls.append(jnp.dot(jnp.concatenate(_split(sp, 2), axis=1), ui, preferred_element_type=F32))
        pair = lambda h: slice(LANES * (h // 2), LANES * (h // 2 + 1))
        new = []
        for h in heads:
            e = jnp.exp(zs[h] - incls[h] - carry[h][1])
            if masked:
                e = jnp.where(causal, e, 0.0)
            new.append((carry[h][0] + jnp.dot(e.astype(BF16), vblk[:, pair(h)], preferred_element_type=F32),
                        carry[h][1] + incls[h][:, 0:1]))
        return tuple(new)

    init = tuple((jnp.zeros((tq, LANES), F32), jnp.zeros((tq, 1), F32)) for _ in heads)
    carry = tile(i, init, True)
    carry = lax.fori_loop(0, i, lambda jj, c: tile(i - 1 - jj, c, False), carry)
    first = lax.broadcasted_iota(jnp.int32, (tq, LANES), 1) < HD_C
    o = jnp.concatenate([jnp.where(first, carry[h][0], carry[h + 1][0]) for h in heads[::2]], axis=-1)
    o_ref[0] = o * _silu(g_ref[0])


def _sb_prompt(q_bf, k_bf, v_bf, g_c, sb_bias):
    bsz, t, _ = q_bf.shape
    tq = min(SB_TILE, t)
    nq = t // tq
    width = SB_HEADS * HD_C
    groups = D_C // width
    bias_lanes = jnp.repeat(sb_bias.astype(F32), HD_C).reshape(groups, 1, width)
    tri = np.tril(np.ones((tq, tq), np.float32))
    ui = jnp.asarray(np.concatenate([tri, tri], axis=0), BF16)
    qspec = pl.BlockSpec((1, tq, width), lambda b, p, i: (b, i, p))
    kvspec = pl.BlockSpec((1, t, width), lambda b, p, i: (b, 0, p))
    return pl.pallas_call(
        _sb_prompt_body,
        grid=(bsz, groups, nq),
        in_specs=[qspec, kvspec, kvspec, qspec,
                  pl.BlockSpec((1, 1, width), lambda b, p, i: (p, 0, 0)),
                  pl.BlockSpec((2 * tq, tq), lambda b, p, i: (0, 0))],
        out_specs=qspec,
        out_shape=jax.ShapeDtypeStruct((bsz, t, D_C), F32),
        compiler_params=_cparams("arbitrary", "arbitrary", "arbitrary"),
        name="sb_prompt",
    )(q_bf, k_bf, v_bf, g_c, bias_lanes, ui)


def _sb_paged_body(pt_ref, qb_ref, g_ref, bias_ref, tri_ref, *refs):
    npg = PAGES_PER_STEP
    k_refs = refs[:npg]
    v_refs = refs[npg:2 * npg]
    o_ref = refs[2 * npg]
    acc, run = refs[2 * npg + 1:]
    j = pl.program_id(1)

    @pl.when(j == 0)
    def _():
        acc[...] = jnp.zeros_like(acc)
        run[...] = jnp.zeros_like(run)

    bias = bias_ref[:, 0:1]
    heads = range(H_C)
    for p in range(npg):
        z = jnp.concatenate([jnp.sum(k_refs[p][0, h] * qb_ref[0, h], axis=0, keepdims=True) for h in heads],
                            axis=0) + bias
        incl = _mm_sel(_softplus(z), tri_ref[...], 2)
        e = jnp.exp(z - incl - run[...])
        for h in heads:
            acc[h] += v_refs[p][0, h] * e[h:h + 1, :]
        run[...] += incl[:, 0:1]

    @pl.when(j == pl.num_programs(1) - 1)
    def _():
        ones = jnp.ones((8, PAGE), BF16)
        rows = []
        for h in heads:
            parts = [lax.dot_general(ones, part, (((1,), (1,)), ((), ())), preferred_element_type=F32)
                     for part in _split(acc[h], 3)]
            rows.append((parts[0] + parts[1] + parts[2])[0:1, :])
        o_ref[0] = jnp.concatenate(rows, axis=0) * _silu(g_ref[0])


def _sb_paged(q_s, g_s, cache_kt, cache_vt, page_table, sb_bias):
    bsz, n_pages = page_table.shape
    npg = PAGES_PER_STEP
    steps = n_pages // npg
    tri = jnp.asarray(np.tril(np.ones((PAGE, PAGE), np.float32)), BF16)
    bias = jnp.broadcast_to(sb_bias.astype(F32)[:, None], (H_C, LANES))
    qb = jnp.broadcast_to(q_s.astype(F32)[..., None], (bsz, H_C, HD_C, PAGE))

    def page_spec(p):
        return pl.BlockSpec((1, H_C, HD_C, PAGE),
                            lambda b, j, pt: (pt[b, n_pages - 1 - (j * npg + p)], 0, 0, 0))

    per_seq = pl.BlockSpec((1, H_C, HD_C), lambda b, j, pt: (b, 0, 0))
    full = lambda shape: pl.BlockSpec(shape, lambda b, j, pt: (0,) * len(shape))
    grid_spec = pltpu.PrefetchScalarGridSpec(
        num_scalar_prefetch=1,
        grid=(bsz, steps),
        in_specs=[pl.BlockSpec((1, H_C, HD_C, PAGE), lambda b, j, pt: (b, 0, 0, 0)), per_seq,
                  full((H_C, LANES)), full((PAGE, PAGE))] + [page_spec(p) for p in range(npg)] * 2,
        out_specs=per_seq,
        scratch_shapes=[pltpu.VMEM((H_C, HD_C, PAGE), F32), pltpu.VMEM((H_C, 1), F32)],
    )
    return pl.pallas_call(
        _sb_paged_body,
        grid_spec=grid_spec,
        out_shape=jax.ShapeDtypeStruct((bsz, H_C, HD_C), F32),
        compiler_params=_cparams("arbitrary", "arbitrary"),
        name="sb_paged",
    )(page_table, qb, g_s, bias, tri, *([cache_kt] * npg), *([cache_vt] * npg))


def _gla_body(q_ref, k_ref, v_ref, la_ref, gd_ref, s0_ref, dmat_ref, gng_ref, o_ref, sf_o, state):
    t_idx = pl.program_id(1)

    @pl.when(t_idx == 0)
    def _():
        state[...] = s0_ref[0]

    q = q_ref[0]
    k = k_ref[0]
    n = q.shape[0]
    dec = jnp.exp(_sel_mm(dmat_ref[...], la_ref[0], 3))
    blk = lambda idx: dec[n * idx:n * (idx + 1), :]
    q_in = q * blk(0)
    k_out = k * blk(1)
    e_last = blk(0)[n - 1:n, :]
    row = lax.broadcasted_iota(jnp.int32, (n, n), 0)
    col = lax.broadcasted_iota(jnp.int32, (n, n), 1)
    rsub = lax.broadcasted_iota(jnp.int32, q.shape, 0)
    levels = _gla_levels(n)
    qk_levels = []
    for li, m in enumerate(levels):
        upper = (rsub & (2 * m - 1)) >= m
        e = blk(2 + li)
        shift = int(math.log2(2 * m))
        qk_levels.append((jnp.where(upper, q * e, 0.0), jnp.where(upper, 0.0, k * e),
                          (row >> shift) == (col >> shift)))
    heads = range(H_D)
    sls = [slice(DK_D * h, DK_D * (h + 1)) for h in heads]
    cast = lambda x: [x[:, sl].astype(BF16) for sl in sls]
    v_all = v_ref[0]
    vh = [v_all[:, DV_D * h:DV_D * (h + 1)].astype(BF16) for h in heads]
    st = [state[h] for h in heads]
    inter = [_mm_nt(qh, s) for qh, s in zip(cast(q_in), st)]
    scores = [jnp.where(row == col, _mm_nt(qh, kh), 0.0) for qh, kh in zip(cast(q), cast(k))]
    for qm, km, same in qk_levels:
        part = [_mm_nt(qh, kh) for qh, kh in zip(cast(qm), cast(km))]
        scores = [scores[h] + jnp.where(same, part[h], 0.0) for h in heads]
    o = [inter[h] + _mm(scores[h], vh[h]) for h in heads]
    k_o = cast(k_out)
    new_st = [st[h] * e_last[:, sls[h]] + _mm_tn(vh[h], k_o[h]) for h in heads]
    for h in heads:
        state[h] = new_st[h]
    outs = [oh * lax.rsqrt(jnp.mean(oh * oh, axis=-1, keepdims=True) + RMS_EPS) for oh in o]
    o_ref[0] = jnp.concatenate(outs, axis=-1) * gng_ref[...] * _silu(gd_ref[0])

    @pl.when(t_idx == pl.num_programs(1) - 1)
    def _():
        sf_o[0] = state[...]


def _gla_scan(q, k, v, la, gd, s0_t, gn_g, chunk):
    bsz, t, hk = q.shape
    nt = t // chunk
    tok = lambda n: pl.BlockSpec((1, chunk, n), lambda b, i: (b, i, 0))
    st = pl.BlockSpec((1, H_D, DV_D, DK_D), lambda b, i: (b, 0, 0, 0))
    full = lambda shape: pl.BlockSpec(shape, lambda b, i: (0,) * len(shape))
    dmat = jnp.asarray(_gla_decay_rows(chunk), BF16)
    return pl.pallas_call(
        _gla_body,
        grid=(bsz, nt),
        in_specs=[tok(hk), tok(hk), tok(D_D), tok(hk), tok(D_D), st, full(dmat.shape), full((1, D_D))],
        out_specs=[tok(D_D), st],
        out_shape=[jax.ShapeDtypeStruct((bsz, t, D_D), F32),
                   jax.ShapeDtypeStruct((bsz, H_D, DV_D, DK_D), F32)],
        scratch_shapes=[pltpu.VMEM((H_D, DV_D, DK_D), F32)],
        compiler_params=_cparams("arbitrary", "arbitrary"),
        name="gla_scan_c%d" % chunk,
    )(q, k, v, la, gd, s0_t, dmat, gn_g.reshape(1, -1))


def _pad_tokens(a, n):
    return jnp.pad(a, ((0, 0), (0, n - a.shape[1]), (0, 0)))


def kernel(x_prompt, x_sample, state_rwkv, state_shift, state_s5_re, state_s5_im, cache_k, cache_v, state_gla,
           page_table, ln_g, ln_b, w_in_e, mu_shift, rwkv_w0, rwkv_w2, rwkv_a0, rwkv_a2, rwkv_k_k, rwkv_k_a,
           rwkv_r_k, rwkv_gn_g, rwkv_gn_b, s5_lambda_re, s5_lambda_im, s5_log_dt, s5_b_re, s5_b_im, s5_c_re,
           s5_c_im, s5_d, s5_glu_w, s5_glu_b, w_out_e, w_in_o, sb_bias, gla_alpha_w, gla_alpha_b, gla_gn_g,
           w_out_o):
    bp, t_p, _ = x_prompt.shape
    bs = x_sample.shape[0]
    n_phys = cache_k.shape[1]
    xs_rows = x_sample.reshape(1, bs, D_MODEL)

    w_in_bf = w_in_e[0].astype(BF16)
    prep_w = (w_in_bf, mu_shift[0], rwkv_w0[0], rwkv_w2[0], rwkv_a0[0], rwkv_a2[0], rwkv_k_k[0], rwkv_k_a[0])
    pe = _even_prep(x_prompt, jnp.zeros((bp, 1, N_SHIFT), F32), *prep_w, decode=False)
    se = _even_prep(xs_rows, state_shift[0].reshape(1, bs, N_SHIFT), *prep_w, decode=True)
    rw_w = (rwkv_gn_g[0], rwkv_gn_b[0], rwkv_r_k[0])
    oa_p, rw_p = _rwkv_scan(*pe[:7], jnp.zeros((bp, H_A, HD_A, HD_A), F32), *rw_w, chunk=RWKV_CHUNK)
    se_tok = [_pad_tokens(a.reshape(bs, 1, D_A), DECODE_CHUNK) for a in se[:7]]
    oa_s, rw_s = _rwkv_scan(*se_tok, state_rwkv[0], *rw_w, chunk=DECODE_CHUNK)

    abar_re, abar_im, bbr, bbi, cr, ci = _s5_params(s5_lambda_re[0], s5_lambda_im[0], s5_log_dt[0], s5_b_re[0],
                                                    s5_b_im[0], s5_c_re[0], s5_c_im[0])
    s5_tail = (cr.astype(BF16), ci.astype(BF16), s5_d[0], s5_glu_w[0].astype(BF16), s5_glu_b[0])
    zst = jnp.zeros((bp, 1, N_STATE_B), F32)
    ob_p, re_p, im_p = _s5_mixer(pe[7], pe[8], zst, zst, abar_re, abar_im, bbr.astype(BF16), bbi.astype(BF16),
                                 *s5_tail, decode=False)
    ob_s, re_s, im_s = _s5_mixer(se[7], se[8], state_s5_re[0].reshape(1, bs, N_STATE_B),
                                 state_s5_im[0].reshape(1, bs, N_STATE_B), abar_re, abar_im, bbr, bbi,
                                 *s5_tail, decode=True)

    w_out_e_bf = w_out_e[0].astype(BF16)
    x1_p = _out_norm(oa_p, ob_p, x_prompt, w_out_e_bf, ln_g[0], ln_b[0])
    x1_s = _out_norm(oa_s[:, 0:1, :].reshape(1, bs, D_A), ob_s, xs_rows, w_out_e_bf, ln_g[0], ln_b[0])

    w = w_in_o[0]
    w_o_bf = jnp.concatenate([w[:, :O_GD], w[:, O_GD + R_G:], w[:, O_GD:O_GD + R_G],
                              jnp.zeros((D_MODEL, N_PROJ_O_STAGED - w.shape[1]), F32)], axis=1).astype(BF16)
    alpha_w_pad = jnp.pad(gla_alpha_w[0], ((0, LANES - R_G), (0, 0)))
    po = _odd_prep(x1_p, w_o_bf, alpha_w_pad, gla_alpha_b[0])
    so = _odd_prep(x1_s, w_o_bf, alpha_w_pad, gla_alpha_b[0])
    qc_p, kc_p, vc_p, kcb_p, vcb_p, gc_p, qd_p, kd_p, vd_p, la_p, gd_p = po
    qc_s, kc_s, vc_s, _, _, gc_s, qd_s, kd_s, vd_s, la_s, gd_s = so

    oc_p = _sb_prompt(qc_p, kcb_p, vcb_p, gc_p, sb_bias[0])
    oc_s = _sb_paged(qc_s.reshape(bs, H_C, HD_C), gc_s.reshape(bs, H_C, HD_C),
                     jnp.transpose(cache_k[0], (0, 2, 3, 1)), jnp.transpose(cache_v[0], (0, 2, 3, 1)),
                     page_table, sb_bias[0])

    od_p, gl_p = _gla_scan(qd_p, kd_p, vd_p, la_p, gd_p, jnp.zeros((bp, H_D, DV_D, DK_D), F32), gla_gn_g[0],
                           chunk=GLA_CHUNK)
    tok_s = [_pad_tokens(a.reshape(bs, 1, a.shape[-1]), DECODE_CHUNK) for a in (qd_s, kd_s, vd_s, la_s, gd_s)]
    od_s, gl_s = _gla_scan(*tok_s, jnp.swapaxes(state_gla[0], -1, -2), gla_gn_g[0], chunk=DECODE_CHUNK)

    w_out_o_bf = w_out_o[0].astype(BF16)
    y_p = _out_norm(oc_p, od_p, x1_p, w_out_o_bf, ln_g[1], ln_b[1])
    y_s = _out_norm(oc_s.reshape(1, bs, D_C), od_s[:, 0:1, :].reshape(1, bs, D_D), x1_s, w_out_o_bf, ln_g[1],
                    ln_b[1])

    heads_c = lambda a, b, t: a.reshape(1, b, t, H_C, HD_C)
    return (y_p, y_s.reshape(bs, 1, D_MODEL),
            rw_p[None], rw_s[None],
            pe[9].reshape(1, bp, N_SHIFT), se[9].reshape(1, bs, N_SHIFT),
            re_p.reshape(1, bp, G_B, P_B), re_s.reshape(1, bs, G_B, P_B),
            im_p.reshape(1, bp, G_B, P_B), im_s.reshape(1, bs, G_B, P_B),
            heads_c(kc_p, bp, t_p), heads_c(kc_s, bs, 1), heads_c(vc_p, bp, t_p), heads_c(vc_s, bs, 1),
            jnp.swapaxes(gl_p, -1, -2)[None], jnp.swapaxes(gl_s, -1, -2)[None])
```

```python
import functools
import math

import numpy as np
import jax
import jax.numpy as jnp
from jax import lax
from jax.experimental import pallas as pl
from jax.experimental.pallas import tpu as pltpu

F32 = jnp.float32
BF16 = jnp.bfloat16

D_MODEL = 1024
D_A = 512
HD_A = 64
H_A = 8
R_W = 64
R_A = 64
D_B = 512
S5_GROUP = 16
G_B = 32
P_B = 64
N_STATE_B = G_B * P_B
D_C = 512
HD_C = 64
H_C = 8
D_D = 512
H_D = 4
DV_D = 128
DK_D = 64
R_G = 16
GLA_TAU = 16.0
N_SHIFT = 3 * D_A + R_W + R_A
N_PROJ_E = N_SHIFT + D_A + 2 * D_B
DEPTH = 2
DEEPNORM_ALPHA = (2.0 * DEPTH) ** 0.25
LN_EPS = 1e-5
GN_EPS = 64e-5
RMS_EPS = 1e-5
PAGE = 128

LOG2E = 1.4426950408889634
LANES = 128
VMEM_LIMIT = 56 * 1024 * 1024

PROJ_BLOCK = 512
S5_BLOCK = 256
RWKV_CHUNK = 64
RWKV_SEQS_PER_STEP = 4
GLA_CHUNK = 64
GLA_SEQS_PER_STEP = 4
DECODE_CHUNK = 8
SB_TILE = 256
SB_HEADS = 8
PAGES_PER_STEP = 8


def _cparams(*sem):
    return pltpu.CompilerParams(dimension_semantics=sem, vmem_limit_bytes=VMEM_LIMIT)


def _mm(a, b):
    return jnp.dot(a.astype(BF16), b.astype(BF16), preferred_element_type=F32)


def _mm_nt(a, b):
    return lax.dot_general(a.astype(BF16), b.astype(BF16), (((1,), (1,)), ((), ())),
                           preferred_element_type=F32)


def _mm_tn(a, b):
    return lax.dot_general(a.astype(BF16), b.astype(BF16), (((0,), (0,)), ((), ())),
                           preferred_element_type=F32)


def _mm_f32(a, b):
    return jnp.dot(a, b, precision=lax.Precision.HIGHEST, preferred_element_type=F32)


def _mmw(a, w):
    if w.dtype == BF16:
        return jnp.dot(a.astype(BF16), w, preferred_element_type=F32)
    return _mm_f32(a, w)


def _split(x, parts):
    out = []
    rem = x
    for _ in range(parts - 1):
        p = rem.astype(BF16)
        out.append(p)
        rem = rem - p.astype(F32)
    out.append(rem.astype(BF16))
    return out


def _sel_mm(sel, x, parts):
    acc = None
    for p in _split(x, parts):
        t = jnp.dot(sel, p, preferred_element_type=F32)
        acc = t if acc is None else acc + t
    return acc


def _mm_sel(x, sel, parts):
    acc = None
    for p in _split(x, parts):
        t = jnp.dot(p, sel, preferred_element_type=F32)
        acc = t if acc is None else acc + t
    return acc


def _sigmoid(t):
    return 1.0 / (1.0 + jnp.exp(-t))


def _silu(t):
    return t * _sigmoid(t)


def _log1pexp_negabs(t):
    return jnp.log(1.0 + jnp.exp2(jnp.abs(t) * (-LOG2E)))


def _softplus(t):
    return jnp.maximum(t, 0.0) + _log1pexp_negabs(t)


def _gelu_tanh(x):
    return 0.5 * x * (1.0 + jnp.tanh(math.sqrt(2.0 / math.pi) * (x + 0.044715 * (x * x * x))))


def _tri_incl(n):
    return np.tril(np.ones((n, n), np.float32))


def _block_ones(n, blk):
    idx = np.arange(n) // blk
    return (idx[:, None] == idx[None, :]).astype(np.float32)


def _gla_decay_rows(n):
    t = np.arange(n)[:, None]
    i = np.arange(n)[None, :]
    mats = [(i <= t), (i > t)]
    m = n // 2
    while m >= 1:
        p = 2 * m * (t // (2 * m)) + m - 1
        upper = (t % (2 * m)) >= m
        mats.append(np.where(upper, (i > p) & (i <= t), (i > t) & (i <= p)))
        m //= 2
    return np.concatenate([x.astype(np.float32) for x in mats], axis=0)


def _gla_levels(n):
    out = []
    m = n // 2
    while m >= 1:
        out.append(m)
        m //= 2
    return out


def _even_prep_body(decode, x_ref, zp_ref, w_ref, mu_ref, w0_ref, w2_ref, a0_ref, a2_ref, kk_ref, ka_ref,
                    ones_ref, r_o, lw_o, k_o, v_o, aa_o, bb_o, ga_o, u_o, gb_o, sh_o, carry):
    x = x_ref[0].astype(BF16)
    proj = jnp.dot(x, w_ref[...], preferred_element_type=F32)
    z = proj[:, :N_SHIFT]
    tb = z.shape[0]
    if decode:
        zp = zp_ref[0]
        sh_o[0] = z
    else:
        @pl.when(pl.program_id(1) == 0)
        def _():
            carry[...] = zp_ref[0]
        row = lax.broadcasted_iota(jnp.int32, z.shape, 0)
        zp = jnp.where(row == 0, carry[...], pltpu.roll(z, 1, 0))
        carry[...] = z[tb - 1:tb, :]
        sh_o[0] = z[tb - 1:tb, :]
    zm = z + mu_ref[...] * (zp - z)
    r = zm[:, 0:D_A]
    k = zm[:, D_A:2 * D_A]
    v = zm[:, 2 * D_A:3 * D_A]
    wd = zm[:, 3 * D_A:3 * D_A + R_W]
    ad = zm[:, 3 * D_A + R_W:N_SHIFT]
    w = -_softplus(-(w0_ref[...] + _mm(jnp.tanh(wd), w2_ref[...]))) - 0.5
    a = _sigmoid(a0_ref[...] + _mm(ad, a2_ref[...]))
    kk = k * kk_ref[...]
    ss = _mm_sel(kk * kk, ones_ref[...], 2)
    kkn = kk / jnp.maximum(jnp.sqrt(ss), 1e-12)
    r_o[0] = r
    lw_o[0] = -jnp.exp(w)
    k_o[0] = k * (1.0 + (a - 1.0) * ka_ref[...])
    v_o[0] = v
    aa_o[0] = -kkn
    bb_o[0] = kkn * a
    ga_o[0] = proj[:, N_SHIFT:N_SHIFT + D_A]
    u_o[0] = proj[:, N_SHIFT + D_A:N_SHIFT + D_A + D_B]
    gb_o[0] = proj[:, N_SHIFT + D_A + D_B:N_PROJ_E]


def _even_prep(x, zprev, w_bf, mu, w0, w2, a0, a2, k_k, k_a, decode):
    bsz, t, _ = x.shape
    tb = t if decode else min(PROJ_BLOCK, t)
    nt = t // tb
    row2 = lambda a: a.reshape(1, -1)
    ones = jnp.asarray(_block_ones(D_A, HD_A), BF16)
    full = lambda shape: pl.BlockSpec(shape, lambda b, i: (0,) * len(shape))
    tok = lambda n: pl.BlockSpec((1, tb, n), lambda b, i: (b, i, 0))
    zp_spec = (pl.BlockSpec((1, tb, N_SHIFT), lambda b, i: (b, i, 0)) if decode
               else pl.BlockSpec((1, 1, N_SHIFT), lambda b, i: (b, 0, 0)))
    sh_rows = tb if decode else 1
    outs = [jax.ShapeDtypeStruct((bsz, t, D_A), F32)] * 9 + [jax.ShapeDtypeStruct((bsz, sh_rows, N_SHIFT), F32)]
    out_specs = [tok(D_A)] * 9 + [pl.BlockSpec((1, sh_rows, N_SHIFT), lambda b, i: (b, 0, 0))]
    return pl.pallas_call(
        functools.partial(_even_prep_body, decode),
        grid=(bsz, nt),
        in_specs=[tok(D_MODEL), zp_spec, full((D_MODEL, N_PROJ_E)), full((1, N_SHIFT)), full((1, D_A)),
                  full((R_W, D_A)), full((1, D_A)), full((R_A, D_A)), full((1, D_A)), full((1, D_A)),
                  full((D_A, D_A))],
        out_specs=out_specs,
        out_shape=outs,
        scratch_shapes=[pltpu.VMEM((1, N_SHIFT), F32)],
        compiler_params=_cparams("arbitrary", "arbitrary"),
        name="even_prep_decode" if decode else "even_prep",
    )(x, zprev, w_bf, row2(mu), row2(w0), w2, row2(a0), a2, row2(k_k), row2(k_a), ones)


def _rwkv_body(r_ref, lw_ref, k_ref, v_ref, aa_ref, bb_ref, ga_ref, s0_ref, tri_ref, ones_ref, gng_ref, gnb_ref,
               rk_ref, oa_o, sf_o, state):
    t_idx = pl.program_id(1)

    @pl.when(t_idx == 0)
    def _():
        state[...] = s0_ref[...]

    nseq, n, _ = lw_ref.shape
    row = lax.broadcasted_iota(jnp.int32, (n, n), 0)
    col = lax.broadcasted_iota(jnp.int32, (n, n), 1)
    strict = col < row
    incl = col <= row
    levels = int(math.log2(n))
    sls = [slice(HD_A * h, HD_A * (h + 1)) for h in range(H_A)]
    cast = lambda x: [x[:, sl].astype(BF16) for sl in sls]

    a_h, b_h, k_h, r_h, v_h, b_lh, k_lh, p_last_h, v_seq, rkr_seq = [], [], [], [], [], [], [], [], [], []
    for s in range(nseq):
        lw = lw_ref[s]
        c = _sel_mm(tri_ref[...], lw, 3)
        c_last = c[n - 1:n, :]
        r = r_ref[s]
        k = k_ref[s]
        v = v_ref[s]
        bb = bb_ref[s]
        pinv = jnp.exp(-c)
        tail = jnp.exp(c_last - c)
        p_last = jnp.exp(c_last)
        a_h += cast(aa_ref[s] * jnp.exp(c - lw))
        b_h += cast(bb * pinv)
        k_h += cast(k * pinv)
        r_h += cast(r * jnp.exp(c))
        v_h += cast(v)
        b_lh += cast(bb * tail)
        k_lh += cast(k * tail)
        p_last_h += [p_last[:, sl] for sl in sls]
        v_seq.append(v)
        rkr_seq.append(r * k * rk_ref[...])
    chains = range(nseq * H_A)
    s_h = [state[c // H_A, c % H_A] for c in chains]
    s_bf = [x.astype(BF16) for x in s_h]
    nmat = [jnp.where(strict, _mm_nt(a_h[c], b_h[c]), 0.0) for c in chains]
    mmat = [jnp.where(strict, _mm_nt(a_h[c], k_h[c]), 0.0) for c in chains]
    rbm = [jnp.where(incl, _mm_nt(r_h[c], b_h[c]), 0.0) for c in chains]
    rkm = [jnp.where(incl, _mm_nt(r_h[c], k_h[c]), 0.0) for c in chains]
    u = [_mm_nt(a_h[c], s_bf[c]) + _mm(mmat[c], v_h[c]) for c in chains]
    y0 = [_mm_nt(r_h[c], s_bf[c]) + _mm(rkm[c], v_h[c]) for c in chains]
    npow = nmat
    for lvl in range(levels):
        u = [u[c] + _mm(npow[c], u[c]) for c in chains]
        if lvl + 1 < levels:
            npow = [_mm(npow[c], npow[c]) for c in chains]
    y = [y0[c] + _mm(rbm[c], u[c]) for c in chains]
    for c in chains:
        state[c // H_A, c % H_A] = s_h[c] * p_last_h[c] + _mm_tn(u[c], b_lh[c]) + _mm_tn(v_h[c], k_lh[c])
    for s in range(nseq):
        yn = []
        for yh in y[s * H_A:(s + 1) * H_A]:
            mu = jnp.mean(yh, axis=-1, keepdims=True)
            var = jnp.mean(jnp.square(yh - mu), axis=-1, keepdims=True)
            yn.append((yh - mu) * lax.rsqrt(var + GN_EPS))
        yn = jnp.concatenate(yn, axis=-1)
        rk_head = _mm_sel(rkr_seq[s], ones_ref[...], 2)
        oa_o[s] = (yn * gng_ref[...] + gnb_ref[...] + rk_head * v_seq[s]) * _silu(ga_ref[s])

    @pl.when(t_idx == pl.num_programs(1) - 1)
    def _():
        sf_o[...] = state[...]


def _rwkv_scan(r, lw, k, v, aa, bb, ga, s0, gn_g, gn_b, r_k, chunk):
    bsz, t, _ = r.shape
    nt = t // chunk
    nseq = RWKV_SEQS_PER_STEP
    tok = pl.BlockSpec((nseq, chunk, D_A), lambda b, i: (b, i, 0))
    st = pl.BlockSpec((nseq, H_A, HD_A, HD_A), lambda b, i: (b, 0, 0, 0))
    full = lambda shape: pl.BlockSpec(shape, lambda b, i: (0,) * len(shape))
    tri = jnp.asarray(_tri_incl(chunk), BF16)
    ones = jnp.asarray(_block_ones(D_A, HD_A), BF16)
    return pl.pallas_call(
        _rwkv_body,
        grid=(bsz // nseq, nt),
        in_specs=[tok] * 7 + [st, full((chunk, chunk)), full((D_A, D_A)), full((1, D_A)), full((1, D_A)),
                  full((1, D_A))],
        out_specs=[tok, st],
        out_shape=[jax.ShapeDtypeStruct((bsz, t, D_A), F32),
                   jax.ShapeDtypeStruct((bsz, H_A, HD_A, HD_A), F32)],
        scratch_shapes=[pltpu.VMEM((nseq, H_A, HD_A, HD_A), F32)],
        compiler_params=_cparams("arbitrary", "arbitrary"),
        name="rwkv_scan_c%d" % chunk,
    )(r, lw, k, v, aa, bb, ga, s0, tri, ones, gn_g.reshape(1, -1), gn_b.reshape(1, -1), r_k.reshape(1, -1))


S5_LANE_BLOCKS = D_B // LANES
S5_STATES_PER_BLOCK = N_STATE_B // S5_LANE_BLOCKS
S5_ROWS = 8


def _s5_body(decode, u_ref, gb_ref, h0r_ref, h0i_ref, ar_ref, ai_ref, bbr_ref, bbi_ref, cr_ref, ci_ref, d_ref,
             gw_ref, gbias_ref, ob_o, hr_o, hi_o, car_r, car_i):
    u = u_ref[0]
    tb = u.shape[0]
    if not decode:
        @pl.when(pl.program_id(1) == 0)
        def _():
            car_r[...] = h0r_ref[0]
            car_i[...] = h0i_ref[0]
        sub = lax.broadcasted_iota(jnp.int32, (tb // S5_ROWS, S5_ROWS, S5_STATES_PER_BLOCK), 1)
    ys = []
    for j in range(S5_LANE_BLOCKS):
        sl = slice(S5_STATES_PER_BLOCK * j, S5_STATES_PER_BLOCK * (j + 1))
        uj = u[:, LANES * j:LANES * (j + 1)]
        bur = _mmw(uj, bbr_ref[j])
        bui = _mmw(uj, bbi_ref[j])
        pw_r = ar_ref[:, sl]
        pw_i = ai_ref[:, sl]
        ar, ai = pw_r[0:1, :], pw_i[0:1, :]
        if decode:
            h0r = h0r_ref[0][:, sl]
            h0i = h0i_ref[0][:, sl]
            hr = ar * h0r - ai * h0i + bur
            hi = ar * h0i + ai * h0r + bui
            hr_o[0, :, sl] = hr
            hi_o[0, :, sl] = hi
        else:
            width = S5_STATES_PER_BLOCK
            groups = tb // S5_ROWS
            hr3 = bur.reshape(groups, S5_ROWS, width)
            hi3 = bui.reshape(groups, S5_ROWS, width)
            pr, pi = ar.reshape(1, 1, width), ai.reshape(1, 1, width)
            s = 1
            while s < S5_ROWS:
                keep = sub >= s
                sr = jnp.where(keep, pltpu.roll(hr3, s, 1), 0.0)
                si = jnp.where(keep, pltpu.roll(hi3, s, 1), 0.0)
                hr3, hi3 = hr3 + pr * sr - pi * si, hi3 + pr * si + pi * sr
                pr, pi = pr * pr - pi * pi, 2.0 * pr * pi
                s *= 2
            cr = car_r[:, sl]
            ci = car_i[:, sl]
            rows_r, rows_i = [], []
            for g in range(groups):
                gr = hr3[g] + pw_r * cr - pw_i * ci
                gi = hi3[g] + pw_r * ci + pw_i * cr
                cr, ci = gr[S5_ROWS - 1:S5_ROWS, :], gi[S5_ROWS - 1:S5_ROWS, :]
                rows_r.append(gr)
                rows_i.append(gi)
            hr = jnp.concatenate(rows_r, axis=0)
            hi = jnp.concatenate(rows_i, axis=0)
            car_r[:, sl] = cr
            car_i[:, sl] = ci
            hr_o[0, :, sl] = cr
            hi_o[0, :, sl] = ci
        ys.append(_mm(hr, cr_ref[j]) - _mm(hi, ci_ref[j]))
    y = jnp.concatenate(ys, axis=-1) + d_ref[...] * u
    y = _gelu_tanh(y)
    y = y * _sigmoid(_mm(y, gw_ref[...]) + gbias_ref[...])
    ob_o[0] = y * _silu(gb_ref[0])


def _s5_mixer(u, gate_b, h0r, h0i, abar_re, abar_im, bbr, bbi, cr, ci, d_skip, glu_w, glu_b, decode):
    bsz, t, _ = u.shape
    tb = t if decode else min(S5_BLOCK, t)
    nt = t // tb
    st_rows = tb if decode else 1
    tok = pl.BlockSpec((1, tb, D_B), lambda b, i: (b, i, 0))
    st = pl.BlockSpec((1, st_rows, N_STATE_B), lambda b, i: (b, 0, 0))
    full = lambda shape: pl.BlockSpec(shape, lambda b, i: (0,) * len(shape))
    return pl.pallas_call(
        functools.partial(_s5_body, decode),
        grid=(bsz, nt),
        in_specs=[tok, tok, st, st, full((S5_ROWS, N_STATE_B)), full((S5_ROWS, N_STATE_B)),
                  full(bbr.shape), full(bbi.shape), full(cr.shape), full(ci.shape),
                  full((1, D_B)), full((D_B, D_B)), full((1, D_B))],
        out_specs=[tok, st, st],
        out_shape=[jax.ShapeDtypeStruct((bsz, t, D_B), F32),
                   jax.ShapeDtypeStruct((bsz, st_rows, N_STATE_B), F32),
                   jax.ShapeDtypeStruct((bsz, st_rows, N_STATE_B), F32)],
        scratch_shapes=[pltpu.VMEM((1, N_STATE_B), F32), pltpu.VMEM((1, N_STATE_B), F32)],
        compiler_params=_cparams("arbitrary", "arbitrary"),
        name="s5_decode" if decode else "s5_scan",
    )(u, gate_b, h0r, h0i, abar_re, abar_im, bbr, bbi, cr, ci, d_skip.reshape(1, -1), glu_w,
      glu_b.reshape(1, -1))


def _s5_params(lam_re, lam_im, log_dt, b_re, b_im, c_re, c_im):
    dt = jnp.exp(log_dt)[:, None]
    mag = jnp.exp(lam_re * dt)
    abar_re, abar_im = mag * jnp.cos(lam_im * dt), mag * jnp.sin(lam_im * dt)
    den = lam_re * lam_re + lam_im * lam_im
    nr, ni = abar_re - 1.0, abar_im
    f_re, f_im = (nr * lam_re + ni * lam_im) / den, (ni * lam_re - nr * lam_im) / den
    bb_re = f_re[..., None] * b_re - f_im[..., None] * b_im
    bb_im = f_re[..., None] * b_im + f_im[..., None] * b_re
    gpb = G_B // S5_LANE_BLOCKS
    eye = jnp.eye(gpb, dtype=F32)

    def stage_b(m):
        m = m.reshape(S5_LANE_BLOCKS, gpb, P_B, S5_GROUP)
        blk = jnp.einsum('jgpc,gh->jgchp', m, eye)
        return blk.reshape(S5_LANE_BLOCKS, gpb * S5_GROUP, gpb * P_B)

    def stage_c(m):
        m = m.reshape(S5_LANE_BLOCKS, gpb, S5_GROUP, P_B)
        blk = jnp.einsum('jgcp,gh->jgphc', m, eye)
        return blk.reshape(S5_LANE_BLOCKS, gpb * P_B, gpb * S5_GROUP)

    pw_re, pw_im = [abar_re.reshape(1, -1)], [abar_im.reshape(1, -1)]
    for _ in range(S5_ROWS - 1):
        pr, pi = pw_re[-1], pw_im[-1]
        pw_re.append(pr * pw_re[0] - pi * pw_im[0])
        pw_im.append(pr * pw_im[0] + pi * pw_re[0])
    return (jnp.concatenate(pw_re, axis=0), jnp.concatenate(pw_im, axis=0), stage_b(bb_re), stage_b(bb_im),
            stage_c(c_re), stage_c(c_im))


def _out_norm_body(a_ref, b_ref, x_ref, w_ref, g_ref, beta_ref, o_ref):
    half = a_ref.shape[-1]
    out = _mmw(a_ref[0], w_ref[0:half, :]) + _mmw(b_ref[0], w_ref[half:2 * half, :])
    h = DEEPNORM_ALPHA * x_ref[0] + out
    mu = jnp.mean(h, axis=-1, keepdims=True)
    var = jnp.mean(jnp.square(h - mu), axis=-1, keepdims=True)
    o_ref[0] = (h - mu) * lax.rsqrt(var + LN_EPS) * g_ref[...] + beta_ref[...]


def _out_norm(a, b, x, w_bf, g, beta):
    bsz, t, half = a.shape
    tb = min(PROJ_BLOCK, t)
    nt = t // tb
    tok = lambda n: pl.BlockSpec((1, tb, n), lambda bi, i: (bi, i, 0))
    full = lambda shape: pl.BlockSpec(shape, lambda bi, i: (0,) * len(shape))
    return pl.pallas_call(
        _out_norm_body,
        grid=(bsz, nt),
        in_specs=[tok(half), tok(half), tok(D_MODEL), full((2 * half, D_MODEL)), full((1, D_MODEL)),
                  full((1, D_MODEL))],
        out_specs=tok(D_MODEL),
        out_shape=jax.ShapeDtypeStruct((bsz, t, D_MODEL), F32),
        compiler_params=_cparams("arbitrary", "arbitrary"),
        name="out_norm",
    )(a, b, x, w_bf, g.reshape(1, -1), beta.reshape(1, -1))


O_QC, O_KC, O_VC, O_GC = 0, 512, 1024, 1536
O_QD, O_KD, O_VD, O_GD, O_AD = 2048, 2304, 2560, 3072, 3584
N_PROJ_O_STAGED = 3712


def _odd_prep_body(x_ref, w_ref, aw_ref, ab_ref, qc_o, kc_o, vc_o, kcb_o, vcb_o, gc_o, qd_o, kd_o, vd_o, la_o,
                   gd_o):
    x = x_ref[0].astype(BF16)
    proj = jnp.dot(x, w_ref[...], preferred_element_type=F32)
    kc = proj[:, O_KC:O_VC]
    vc = proj[:, O_VC:O_GC]
    qc_o[0] = (proj[:, O_QC:O_KC] * (HD_C ** -0.5)).astype(BF16)
    kc_o[0] = kc
    vc_o[0] = vc
    kcb_o[0] = kc.astype(BF16)
    vcb_o[0] = vc.astype(BF16)
    gc_o[0] = proj[:, O_GC:O_QD]
    qd_o[0] = proj[:, O_QD:O_KD] * (DK_D ** -0.5)
    kd_o[0] = proj[:, O_KD:O_VD]
    vd_o[0] = proj[:, O_VD:O_GD]
    gd_o[0] = proj[:, O_GD:O_AD]
    pre = _mm(proj[:, O_AD:N_PROJ_O_STAGED], aw_ref[...]) + ab_ref[...]
    la_o[0] = -_softplus(-pre) * (1.0 / GLA_TAU)


def _odd_prep(x, w_bf, alpha_w_pad, alpha_b):
    bsz, t, _ = x.shape
    tb = min(PROJ_BLOCK, t)
    nt = t // tb
    tok = lambda n: pl.BlockSpec((1, tb, n), lambda b, i: (b, i, 0))
    full = lambda shape: pl.BlockSpec(shape, lambda b, i: (0,) * len(shape))
    hk = H_D * DK_D
    widths = [(D_C, BF16), (D_C, F32), (D_C, F32), (D_C, BF16), (D_C, BF16), (D_C, F32),
              (hk, F32), (hk, F32), (D_D, F32), (hk, F32), (D_D, F32)]
    return pl.pallas_call(
        _odd_prep_body,
        grid=(bsz, nt),
        in_specs=[tok(D_MODEL), full((D_MODEL, N_PROJ_O_STAGED)), full((LANES, hk)), full((1, hk))],
        out_specs=[tok(n) for n, _ in widths],
        out_shape=[jax.ShapeDtypeStruct((bsz, t, n), dt) for n, dt in widths],
        compiler_params=_cparams("arbitrary", "arbitrary"),
        name="odd_prep",
    )(x, w_bf, alpha_w_pad, alpha_b.reshape(1, -1))


def _sb_prompt_body(q_ref, k_ref, v_ref, g_ref, bias_ref, ui_ref, o_ref):
    i = pl.program_id(2)
    q = q_ref[0]
    tq, width = q.shape
    heads = range(width // HD_C)
    lane = lax.broadcasted_iota(jnp.int32, q.shape, 1)
    zero = jnp.zeros_like(q)
    qs = [jnp.where((lane >= HD_C * h) & (lane < HD_C * (h + 1)), q, zero) for h in heads]
    bias = bias_ref[0]
    biases = [bias[:, HD_C * h:HD_C * h + 1] for h in heads]
    ui = ui_ref[...]
    row = lax.broadcasted_iota(jnp.int32, (tq, tq), 0)
    col = lax.broadcasted_iota(jnp.int32, (tq, tq), 1)
    causal = col < row

    def tile(j, carry, masked):
        start = pl.multiple_of(j * tq, tq)
        kblk = k_ref[0, pl.ds(start, tq), :]
        vblk = v_ref[0, pl.ds(start, tq), :]
        zs = [lax.dot_general(qs[h], kblk, (((1,), (1,)), ((), ())), preferred_element_type=F32) + biases[h]
              for h in heads]
        incls = []
        for z in zs:
            sp = _softplus(z)
            if masked:
                sp = jnp.where(causal, sp, 0.0)
            incls.append(jnp.dot(jnp.concatenate(_split(sp, 2), axis=1), ui, preferred_element_type=F32))
        pair = lambda h: slice(LANES * (h // 2), LANES * (h // 2 + 1))
        new = []
        for h in heads:
            e = jnp.exp(zs[h] - incls[h] - carry[h][1])
            if masked:
                e = jnp.where(causal, e, 0.0)
            new.append((carry[h][0] + jnp.dot(e.astype(BF16), vblk[:, pair(h)], preferred_element_type=F32),
                        carry[h][1] + incls[h][:, 0:1]))
        return tuple(new)

    init = tuple((jnp.zeros((tq, LANES), F32), jnp.zeros((tq, 1), F32)) for _ in heads)
    carry = tile(i, init, True)
    carry = lax.fori_loop(0, i, lambda jj, c: tile(i - 1 - jj, c, False), carry)
    first = lax.broadcasted_iota(jnp.int32, (tq, LANES), 1) < HD_C
    o = jnp.concatenate([jnp.where(first, carry[h][0], carry[h + 1][0]) for h in heads[::2]], axis=-1)
    o_ref[0] = o * _silu(g_ref[0])


def _sb_prompt(q_bf, k_bf, v_bf, g_c, sb_bias):
    bsz, t, _ = q_bf.shape
    tq = min(SB_TILE, t)
    nq = t // tq
    width = SB_HEADS * HD_C
    groups = D_C // width
    bias_lanes = jnp.repeat(sb_bias.astype(F32), HD_C).reshape(groups, 1, width)
    tri = np.tril(np.ones((tq, tq), np.float32))
    ui = jnp.asarray(np.concatenate([tri, tri], axis=0), BF16)
    qspec = pl.BlockSpec((1, tq, width), lambda b, p, i: (b, i, p))
    kvspec = pl.BlockSpec((1, t, width), lambda b, p, i: (b, 0, p))
    return pl.pallas_call(
        _sb_prompt_body,
        grid=(bsz, groups, nq),
        in_specs=[qspec, kvspec, kvspec, qspec,
                  pl.BlockSpec((1, 1, width), lambda b, p, i: (p, 0, 0)),
                  pl.BlockSpec((2 * tq, tq), lambda b, p, i: (0, 0))],
        out_specs=qspec,
        out_shape=jax.ShapeDtypeStruct((bsz, t, D_C), F32),
        compiler_params=_cparams("arbitrary", "arbitrary", "arbitrary"),
        name="sb_prompt",
    )(q_bf, k_bf, v_bf, g_c, bias_lanes, ui)


def _sb_paged_body(pt_ref, qb_ref, g_ref, bias_ref, tri_ref, *refs):
    npg = PAGES_PER_STEP
    k_refs = refs[:npg]
    v_refs = refs[npg:2 * npg]
    o_ref = refs[2 * npg]
    acc, run = refs[2 * npg + 1:]
    j = pl.program_id(1)

    @pl.when(j == 0)
    def _():
        acc[...] = jnp.zeros_like(acc)
        run[...] = jnp.zeros_like(run)

    bias = bias_ref[:, 0:1]
    heads = range(H_C)
    for p in range(npg):
        z = jnp.concatenate([jnp.sum(k_refs[p][0, h] * qb_ref[0, h], axis=0, keepdims=True) for h in heads],
                            axis=0) + bias
        incl = _mm_sel(_softplus(z), tri_ref[...], 2)
        e = jnp.exp(z - incl - run[...])
        for h in heads:
            acc[h] += v_refs[p][0, h] * e[h:h + 1, :]
        run[...] += incl[:, 0:1]

    @pl.when(j == pl.num_programs(1) - 1)
    def _():
        ones = jnp.ones((8, PAGE), BF16)
        rows = []
        for h in heads:
            parts = [lax.dot_general(ones, part, (((1,), (1,)), ((), ())), preferred_element_type=F32)
                     for part in _split(acc[h], 3)]
            rows.append((parts[0] + parts[1] + parts[2])[0:1, :])
        o_ref[0] = jnp.concatenate(rows, axis=0) * _silu(g_ref[0])


def _sb_paged(q_s, g_s, cache_kt, cache_vt, page_table, sb_bias):
    bsz, n_pages = page_table.shape
    npg = PAGES_PER_STEP
    steps = n_pages // npg
    tri = jnp.asarray(np.tril(np.ones((PAGE, PAGE), np.float32)), BF16)
    bias = jnp.broadcast_to(sb_bias.astype(F32)[:, None], (H_C, LANES))
    qb = jnp.broadcast_to(q_s.astype(F32)[..., None], (bsz, H_C, HD_C, PAGE))

    def page_spec(p):
        return pl.BlockSpec((1, H_C, HD_C, PAGE),
                            lambda b, j, pt: (pt[b, n_pages - 1 - (j * npg + p)], 0, 0, 0))

    per_seq = pl.BlockSpec((1, H_C, HD_C), lambda b, j, pt: (b, 0, 0))
    full = lambda shape: pl.BlockSpec(shape, lambda b, j, pt: (0,) * len(shape))
    grid_spec = pltpu.PrefetchScalarGridSpec(
        num_scalar_prefetch=1,
        grid=(bsz, steps),
        in_specs=[pl.BlockSpec((1, H_C, HD_C, PAGE), lambda b, j, pt: (b, 0, 0, 0)), per_seq,
                  full((H_C, LANES)), full((PAGE, PAGE))] + [page_spec(p) for p in range(npg)] * 2,
        out_specs=per_seq,
        scratch_shapes=[pltpu.VMEM((H_C, HD_C, PAGE), F32), pltpu.VMEM((H_C, 1), F32)],
    )
    return pl.pallas_call(
        _sb_paged_body,
        grid_spec=grid_spec,
        out_shape=jax.ShapeDtypeStruct((bsz, H_C, HD_C), F32),
        compiler_params=_cparams("arbitrary", "arbitrary"),
        name="sb_paged",
    )(page_table, qb, g_s, bias, tri, *([cache_kt] * npg), *([cache_vt] * npg))


def _gla_body(q_ref, k_ref, v_ref, la_ref, gd_ref, s0_ref, dmat_ref, gng_ref, o_ref, sf_o, state):
    t_idx = pl.program_id(1)

    @pl.when(t_idx == 0)
    def _():
        state[...] = s0_ref[...]

    nseq, n, _ = q_ref.shape
    row = lax.broadcasted_iota(jnp.int32, (n, n), 0)
    col = lax.broadcasted_iota(jnp.int32, (n, n), 1)
    rsub = lax.broadcasted_iota(jnp.int32, (n, H_D * DK_D), 0)
    levels = _gla_levels(n)
    same_block = [(row >> int(math.log2(2 * m))) == (col >> int(math.log2(2 * m))) for m in levels]
    sls = [slice(DK_D * h, DK_D * (h + 1)) for h in range(H_D)]
    cast = lambda x: [x[:, sl].astype(BF16) for sl in sls]

    q_h, k_h, q_in, k_out, e_last, vh = [], [], [], [], [], []
    q_lv = [[] for _ in levels]
    k_lv = [[] for _ in levels]
    for s in range(nseq):
        q = q_ref[s]
        k = k_ref[s]
        dec = jnp.exp(_sel_mm(dmat_ref[...], la_ref[s], 3))
        blk = lambda idx: dec[n * idx:n * (idx + 1), :]
        q_h += cast(q)
        k_h += cast(k)
        q_in += cast(q * blk(0))
        k_out += cast(k * blk(1))
        e_last += [blk(0)[n - 1:n, sl] for sl in sls]
        for li, m in enumerate(levels):
            upper = (rsub & (2 * m - 1)) >= m
            e = blk(2 + li)
            q_lv[li] += cast(jnp.where(upper, q * e, 0.0))
            k_lv[li] += cast(jnp.where(upper, 0.0, k * e))
        v_all = v_ref[s]
        vh += [v_all[:, DV_D * h:DV_D * (h + 1)].astype(BF16) for h in range(H_D)]
    chains = range(nseq * H_D)
    st = [state[c // H_D, c % H_D] for c in chains]
    inter = [_mm_nt(q_in[c], st[c]) for c in chains]
    scores = [jnp.where(row == col, _mm_nt(q_h[c], k_h[c]), 0.0) for c in chains]
    for li in range(len(levels)):
        part = [_mm_nt(q_lv[li][c], k_lv[li][c]) for c in chains]
        scores = [scores[c] + jnp.where(same_block[li], part[c], 0.0) for c in chains]
    o = [inter[c] + _mm(scores[c], vh[c]) for c in chains]
    for c in chains:
        state[c // H_D, c % H_D] = st[c] * e_last[c] + _mm_tn(vh[c], k_out[c])
    outs = [oc * lax.rsqrt(jnp.mean(oc * oc, axis=-1, keepdims=True) + RMS_EPS) for oc in o]
    for s in range(nseq):
        o_ref[s] = jnp.concatenate(outs[s * H_D:(s + 1) * H_D], axis=-1) * gng_ref[...] * _silu(gd_ref[s])

    @pl.when(t_idx == pl.num_programs(1) - 1)
    def _():
        sf_o[...] = state[...]


def _gla_scan(q, k, v, la, gd, s0_t, gn_g, chunk):
    bsz, t, hk = q.shape
    nt = t // chunk
    nseq = GLA_SEQS_PER_STEP
    tok = lambda n: pl.BlockSpec((nseq, chunk, n), lambda b, i: (b, i, 0))
    st = pl.BlockSpec((nseq, H_D, DV_D, DK_D), lambda b, i: (b, 0, 0, 0))
    full = lambda shape: pl.BlockSpec(shape, lambda b, i: (0,) * len(shape))
    dmat = jnp.asarray(_gla_decay_rows(chunk), BF16)
    return pl.pallas_call(
        _gla_body,
        grid=(bsz // nseq, nt),
        in_specs=[tok(hk), tok(hk), tok(D_D), tok(hk), tok(D_D), st, full(dmat.shape), full((1, D_D))],
        out_specs=[tok(D_D), st],
        out_shape=[jax.ShapeDtypeStruct((bsz, t, D_D), F32),
                   jax.ShapeDtypeStruct((bsz, H_D, DV_D, DK_D), F32)],
        scratch_shapes=[pltpu.VMEM((nseq, H_D, DV_D, DK_D), F32)],
        compiler_params=_cparams("arbitrary", "arbitrary"),
        name="gla_scan_c%d" % chunk,
    )(q, k, v, la, gd, s0_t, dmat, gn_g.reshape(1, -1))


def _pad_tokens(a, n):
    return jnp.pad(a, ((0, 0), (0, n - a.shape[1]), (0, 0)))


def kernel(x_prompt, x_sample, state_rwkv, state_shift, state_s5_re, state_s5_im, cache_k, cache_v, state_gla,
           page_table, ln_g, ln_b, w_in_e, mu_shift, rwkv_w0, rwkv_w2, rwkv_a0, rwkv_a2, rwkv_k_k, rwkv_k_a,
           rwkv_r_k, rwkv_gn_g, rwkv_gn_b, s5_lambda_re, s5_lambda_im, s5_log_dt, s5_b_re, s5_b_im, s5_c_re,
           s5_c_im, s5_d, s5_glu_w, s5_glu_b, w_out_e, w_in_o, sb_bias, gla_alpha_w, gla_alpha_b, gla_gn_g,
           w_out_o):
    bp, t_p, _ = x_prompt.shape
    bs = x_sample.shape[0]
    n_phys = cache_k.shape[1]
    xs_rows = x_sample.reshape(1, bs, D_MODEL)

    w_in_bf = w_in_e[0].astype(BF16)
    prep_w = (w_in_bf, mu_shift[0], rwkv_w0[0], rwkv_w2[0], rwkv_a0[0], rwkv_a2[0], rwkv_k_k[0], rwkv_k_a[0])
    pe = _even_prep(x_prompt, jnp.zeros((bp, 1, N_SHIFT), F32), *prep_w, decode=False)
    se = _even_prep(xs_rows, state_shift[0].reshape(1, bs, N_SHIFT), *prep_w, decode=True)
    rw_w = (rwkv_gn_g[0], rwkv_gn_b[0], rwkv_r_k[0])
    oa_p, rw_p = _rwkv_scan(*pe[:7], jnp.zeros((bp, H_A, HD_A, HD_A), F32), *rw_w, chunk=RWKV_CHUNK)
    se_tok = [_pad_tokens(a.reshape(bs, 1, D_A), DECODE_CHUNK) for a in se[:7]]
    oa_s, rw_s = _rwkv_scan(*se_tok, state_rwkv[0], *rw_w, chunk=DECODE_CHUNK)

    abar_re, abar_im, bbr, bbi, cr, ci = _s5_params(s5_lambda_re[0], s5_lambda_im[0], s5_log_dt[0], s5_b_re[0],
                                                    s5_b_im[0], s5_c_re[0], s5_c_im[0])
    s5_tail = (cr.astype(BF16), ci.astype(BF16), s5_d[0], s5_glu_w[0].astype(BF16), s5_glu_b[0])
    zst = jnp.zeros((bp, 1, N_STATE_B), F32)
    ob_p, re_p, im_p = _s5_mixer(pe[7], pe[8], zst, zst, abar_re, abar_im, bbr.astype(BF16), bbi.astype(BF16),
                                 *s5_tail, decode=False)
    ob_s, re_s, im_s = _s5_mixer(se[7], se[8], state_s5_re[0].reshape(1, bs, N_STATE_B),
                                 state_s5_im[0].reshape(1, bs, N_STATE_B), abar_re, abar_im, bbr, bbi,
                                 *s5_tail, decode=True)

    w_out_e_bf = w_out_e[0].astype(BF16)
    x1_p = _out_norm(oa_p, ob_p, x_prompt, w_out_e_bf, ln_g[0], ln_b[0])
    x1_s = _out_norm(oa_s[:, 0:1, :].reshape(1, bs, D_A), ob_s, xs_rows, w_out_e_bf, ln_g[0], ln_b[0])

    w = w_in_o[0]
    w_o_bf = jnp.concatenate([w[:, :O_GD], w[:, O_GD + R_G:], w[:, O_GD:O_GD + R_G],
                              jnp.zeros((D_MODEL, N_PROJ_O_STAGED - w.shape[1]), F32)], axis=1).astype(BF16)
    alpha_w_pad = jnp.pad(gla_alpha_w[0], ((0, LANES - R_G), (0, 0)))
    po = _odd_prep(x1_p, w_o_bf, alpha_w_pad, gla_alpha_b[0])
    so = _odd_prep(x1_s, w_o_bf, alpha_w_pad, gla_alpha_b[0])
    qc_p, kc_p, vc_p, kcb_p, vcb_p, gc_p, qd_p, kd_p, vd_p, la_p, gd_p = po
    qc_s, kc_s, vc_s, _, _, gc_s, qd_s, kd_s, vd_s, la_s, gd_s = so

    oc_p = _sb_prompt(qc_p, kcb_p, vcb_p, gc_p, sb_bias[0])
    oc_s = _sb_paged(qc_s.reshape(bs, H_C, HD_C), gc_s.reshape(bs, H_C, HD_C),
                     jnp.transpose(cache_k[0], (0, 2, 3, 1)), jnp.transpose(cache_v[0], (0, 2, 3, 1)),
                     page_table, sb_bias[0])

    od_p, gl_p = _gla_scan(qd_p, kd_p, vd_p, la_p, gd_p, jnp.zeros((bp, H_D, DV_D, DK_D), F32), gla_gn_g[0],
                           chunk=GLA_CHUNK)
    tok_s = [_pad_tokens(a.reshape(bs, 1, a.shape[-1]), DECODE_CHUNK) for a in (qd_s, kd_s, vd_s, la_s, gd_s)]
    od_s, gl_s = _gla_scan(*tok_s, jnp.swapaxes(state_gla[0], -1, -2), gla_gn_g[0], chunk=DECODE_CHUNK)

    w_out_o_bf = w_out_o[0].astype(BF16)
    y_p = _out_norm(oc_p, od_p, x1_p, w_out_o_bf, ln_g[1], ln_b[1])
    y_s = _out_norm(oc_s.reshape(1, bs, D_C), od_s[:, 0:1, :].reshape(1, bs, D_D), x1_s, w_out_o_bf, ln_g[1],
                    ln_b[1])

    heads_c = lambda a, b, t: a.reshape(1, b, t, H_C, HD_C)
    return (y_p, y_s.reshape(bs, 1, D_MODEL),
            rw_p[None], rw_s[None],
            pe[9].reshape(1, bp, N_SHIFT), se[9].reshape(1, bs, N_SHIFT),
            re_p.reshape(1, bp, G_B, P_B), re_s.reshape(1, bs, G_B, P_B),
            im_p.reshape(1, bp, G_B, P_B), im_s.reshape(1, bs, G_B, P_B),
            heads_c(kc_p, bp, t_p), heads_c(kc_s, bs, 1), heads_c(vc_p, bp, t_p), heads_c(vc_s, bs, 1),
            jnp.swapaxes(gl_p, -1, -2)[None], jnp.swapaxes(gl_s, -1, -2)[None])
```

```python
import functools
import math

import numpy as np
import jax
import jax.numpy as jnp
from jax import lax
from jax.experimental import pallas as pl
from jax.experimental.pallas import tpu as pltpu

F32 = jnp.float32
BF16 = jnp.bfloat16

D_MODEL = 1024
D_A = 512
HD_A = 64
H_A = 8
R_W = 64
R_A = 64
D_B = 512
S5_GROUP = 16
G_B = 32
P_B = 64
N_STATE_B = G_B * P_B
D_C = 512
HD_C = 64
H_C = 8
D_D = 512
H_D = 4
DV_D = 128
DK_D = 64
R_G = 16
GLA_TAU = 16.0
N_SHIFT = 3 * D_A + R_W + R_A
N_PROJ_E = N_SHIFT + D_A + 2 * D_B
DEPTH = 2
DEEPNORM_ALPHA = (2.0 * DEPTH) ** 0.25
LN_EPS = 1e-5
GN_EPS = 64e-5
RMS_EPS = 1e-5
PAGE = 128

LOG2E = 1.4426950408889634
LANES = 128
VMEM_LIMIT = 56 * 1024 * 1024

PROJ_BLOCK = 512
S5_BLOCK = 256
RWKV_CHUNK = 64
RWKV_SEQS_PER_STEP = 4
GLA_CHUNK = 64
GLA_SEQS_PER_STEP = 4
DECODE_CHUNK = 8
SB_TILE = 256
SB_HEADS = 8
PAGES_PER_STEP = 16


def _cparams(*sem):
    return pltpu.CompilerParams(dimension_semantics=sem, vmem_limit_bytes=VMEM_LIMIT)


def _mm(a, b):
    return jnp.dot(a.astype(BF16), b.astype(BF16), preferred_element_type=F32)


def _mm_nt(a, b):
    return lax.dot_general(a.astype(BF16), b.astype(BF16), (((1,), (1,)), ((), ())),
                           preferred_element_type=F32)


def _mm_tn(a, b):
    return lax.dot_general(a.astype(BF16), b.astype(BF16), (((0,), (0,)), ((), ())),
                           preferred_element_type=F32)


def _mm_f32(a, b):
    return jnp.dot(a, b, precision=lax.Precision.HIGHEST, preferred_element_type=F32)


def _mmw(a, w):
    if w.dtype == BF16:
        return jnp.dot(a.astype(BF16), w, preferred_element_type=F32)
    return _mm_f32(a, w)


def _split(x, parts):
    out = []
    rem = x
    for _ in range(parts - 1):
        p = rem.astype(BF16)
        out.append(p)
        rem = rem - p.astype(F32)
    out.append(rem.astype(BF16))
    return out


def _sel_mm(sel, x, parts):
    acc = None
    for p in _split(x, parts):
        t = jnp.dot(sel, p, preferred_element_type=F32)
        acc = t if acc is None else acc + t
    return acc


def _mm_sel(x, sel, parts):
    acc = None
    for p in _split(x, parts):
        t = jnp.dot(p, sel, preferred_element_type=F32)
        acc = t if acc is None else acc + t
    return acc


def _sigmoid(t):
    return 1.0 / (1.0 + jnp.exp(-t))


def _silu(t):
    return t * _sigmoid(t)


def _log1pexp_negabs(t):
    return jnp.log(1.0 + jnp.exp2(jnp.abs(t) * (-LOG2E)))


def _softplus(t):
    return jnp.maximum(t, 0.0) + _log1pexp_negabs(t)


def _gelu_tanh(x):
    return 0.5 * x * (1.0 + jnp.tanh(math.sqrt(2.0 / math.pi) * (x + 0.044715 * (x * x * x))))


def _tri_incl(n):
    return np.tril(np.ones((n, n), np.float32))


def _block_ones(n, blk):
    idx = np.arange(n) // blk
    return (idx[:, None] == idx[None, :]).astype(np.float32)


def _gla_decay_rows(n):
    t = np.arange(n)[:, None]
    i = np.arange(n)[None, :]
    mats = [(i <= t), (i > t)]
    m = n // 2
    while m >= 1:
        p = 2 * m * (t // (2 * m)) + m - 1
        upper = (t % (2 * m)) >= m
        mats.append(np.where(upper, (i > p) & (i <= t), (i > t) & (i <= p)))
        m //= 2
    return np.concatenate([x.astype(np.float32) for x in mats], axis=0)


def _gla_levels(n):
    out = []
    m = n // 2
    while m >= 1:
        out.append(m)
        m //= 2
    return out


def _even_prep_body(decode, x_ref, zp_ref, w_ref, mu_ref, w0_ref, w2_ref, a0_ref, a2_ref, kk_ref, ka_ref,
                    ones_ref, r_o, lw_o, k_o, v_o, aa_o, bb_o, ga_o, u_o, gb_o, sh_o, carry):
    x = x_ref[0].astype(BF16)
    proj = jnp.dot(x, w_ref[...], preferred_element_type=F32)
    z = proj[:, :N_SHIFT]
    tb = z.shape[0]
    if decode:
        zp = zp_ref[0]
        sh_o[0] = z
    else:
        @pl.when(pl.program_id(1) == 0)
        def _():
            carry[...] = zp_ref[0]
        row = lax.broadcasted_iota(jnp.int32, z.shape, 0)
        zp = jnp.where(row == 0, carry[...], pltpu.roll(z, 1, 0))
        carry[...] = z[tb - 1:tb, :]
        sh_o[0] = z[tb - 1:tb, :]
    zm = z + mu_ref[...] * (zp - z)
    r = zm[:, 0:D_A]
    k = zm[:, D_A:2 * D_A]
    v = zm[:, 2 * D_A:3 * D_A]
    wd = zm[:, 3 * D_A:3 * D_A + R_W]
    ad = zm[:, 3 * D_A + R_W:N_SHIFT]
    w = -_softplus(-(w0_ref[...] + _mm(jnp.tanh(wd), w2_ref[...]))) - 0.5
    a = _sigmoid(a0_ref[...] + _mm(ad, a2_ref[...]))
    kk = k * kk_ref[...]
    ss = _mm_sel(kk * kk, ones_ref[...], 2)
    kkn = kk / jnp.maximum(jnp.sqrt(ss), 1e-12)
    r_o[0] = r
    lw_o[0] = -jnp.exp(w)
    k_o[0] = k * (1.0 + (a - 1.0) * ka_ref[...])
    v_o[0] = v
    aa_o[0] = -kkn
    bb_o[0] = kkn * a
    ga_o[0] = proj[:, N_SHIFT:N_SHIFT + D_A]
    u_o[0] = proj[:, N_SHIFT + D_A:N_SHIFT + D_A + D_B]
    gb_o[0] = proj[:, N_SHIFT + D_A + D_B:N_PROJ_E]


def _even_prep(x, zprev, w_bf, mu, w0, w2, a0, a2, k_k, k_a, decode):
    bsz, t, _ = x.shape
    tb = t if decode else min(PROJ_BLOCK, t)
    nt = t // tb
    row2 = lambda a: a.reshape(1, -1)
    ones = jnp.asarray(_block_ones(D_A, HD_A), BF16)
    full = lambda shape: pl.BlockSpec(shape, lambda b, i: (0,) * len(shape))
    tok = lambda n: pl.BlockSpec((1, tb, n), lambda b, i: (b, i, 0))
    zp_spec = (pl.BlockSpec((1, tb, N_SHIFT), lambda b, i: (b, i, 0)) if decode
               else pl.BlockSpec((1, 1, N_SHIFT), lambda b, i: (b, 0, 0)))
    sh_rows = tb if decode else 1
    outs = [jax.ShapeDtypeStruct((bsz, t, D_A), F32)] * 9 + [jax.ShapeDtypeStruct((bsz, sh_rows, N_SHIFT), F32)]
    out_specs = [tok(D_A)] * 9 + [pl.BlockSpec((1, sh_rows, N_SHIFT), lambda b, i: (b, 0, 0))]
    return pl.pallas_call(
        functools.partial(_even_prep_body, decode),
        grid=(bsz, nt),
        in_specs=[tok(D_MODEL), zp_spec, full((D_MODEL, N_PROJ_E)), full((1, N_SHIFT)), full((1, D_A)),
                  full((R_W, D_A)), full((1, D_A)), full((R_A, D_A)), full((1, D_A)), full((1, D_A)),
                  full((D_A, D_A))],
        out_specs=out_specs,
        out_shape=outs,
        scratch_shapes=[pltpu.VMEM((1, N_SHIFT), F32)],
        compiler_params=_cparams("arbitrary", "arbitrary"),
        name="even_prep_decode" if decode else "even_prep",
    )(x, zprev, w_bf, row2(mu), row2(w0), w2, row2(a0), a2, row2(k_k), row2(k_a), ones)


def _rwkv_body(r_ref, lw_ref, k_ref, v_ref, aa_ref, bb_ref, ga_ref, s0_ref, tri_ref, ones_ref, gng_ref, gnb_ref,
               rk_ref, oa_o, sf_o, state):
    t_idx = pl.program_id(1)

    @pl.when(t_idx == 0)
    def _():
        state[...] = s0_ref[...]

    nseq, n, _ = lw_ref.shape
    row = lax.broadcasted_iota(jnp.int32, (n, n), 0)
    col = lax.broadcasted_iota(jnp.int32, (n, n), 1)
    strict = col < row
    incl = col <= row
    levels = int(math.log2(n))
    sls = [slice(HD_A * h, HD_A * (h + 1)) for h in range(H_A)]
    cast = lambda x: [x[:, sl].astype(BF16) for sl in sls]

    a_h, b_h, k_h, r_h, v_h, b_lh, k_lh, p_last_h, v_seq, rkr_seq = [], [], [], [], [], [], [], [], [], []
    for s in range(nseq):
        lw = lw_ref[s]
        c = _sel_mm(tri_ref[...], lw, 3)
        c_last = c[n - 1:n, :]
        r = r_ref[s]
        k = k_ref[s]
        v = v_ref[s]
        bb = bb_ref[s]
        pinv = jnp.exp(-c)
        tail = jnp.exp(c_last - c)
        p_last = jnp.exp(c_last)
        a_h += cast(aa_ref[s] * jnp.exp(c - lw))
        b_h += cast(bb * pinv)
        k_h += cast(k * pinv)
        r_h += cast(r * jnp.exp(c))
        v_h += cast(v)
        b_lh += cast(bb * tail)
        k_lh += cast(k * tail)
        p_last_h += [p_last[:, sl] for sl in sls]
        v_seq.append(v)
        rkr_seq.append(r * k * rk_ref[...])
    chains = range(nseq * H_A)
    s_h = [state[c // H_A, c % H_A] for c in chains]
    s_bf = [x.astype(BF16) for x in s_h]
    nmat = [jnp.where(strict, _mm_nt(a_h[c], b_h[c]), 0.0) for c in chains]
    mmat = [jnp.where(strict, _mm_nt(a_h[c], k_h[c]), 0.0) for c in chains]
    rbm = [jnp.where(incl, _mm_nt(r_h[c], b_h[c]), 0.0) for c in chains]
    rkm = [jnp.where(incl, _mm_nt(r_h[c], k_h[c]), 0.0) for c in chains]
    u = [_mm_nt(a_h[c], s_bf[c]) + _mm(mmat[c], v_h[c]) for c in chains]
    y0 = [_mm_nt(r_h[c], s_bf[c]) + _mm(rkm[c], v_h[c]) for c in chains]
    npow = nmat
    for lvl in range(levels):
        u = [u[c] + _mm(npow[c], u[c]) for c in chains]
        if lvl + 1 < levels:
            npow = [_mm(npow[c], npow[c]) for c in chains]
    y = [y0[c] + _mm(rbm[c], u[c]) for c in chains]
    for c in chains:
        state[c // H_A, c % H_A] = s_h[c] * p_last_h[c] + _mm_tn(u[c], b_lh[c]) + _mm_tn(v_h[c], k_lh[c])
    for s in range(nseq):
        yn = []
        for yh in y[s * H_A:(s + 1) * H_A]:
            mu = jnp.mean(yh, axis=-1, keepdims=True)
            var = jnp.mean(jnp.square(yh - mu), axis=-1, keepdims=True)
            yn.append((yh - mu) * lax.rsqrt(var + GN_EPS))
        yn = jnp.concatenate(yn, axis=-1)
        rk_head = _mm_sel(rkr_seq[s], ones_ref[...], 2)
        oa_o[s] = (yn * gng_ref[...] + gnb_ref[...] + rk_head * v_seq[s]) * _silu(ga_ref[s])

    @pl.when(t_idx == pl.num_programs(1) - 1)
    def _():
        sf_o[...] = state[...]


def _rwkv_scan(r, lw, k, v, aa, bb, ga, s0, gn_g, gn_b, r_k, chunk):
    bsz, t, _ = r.shape
    nt = t // chunk
    nseq = RWKV_SEQS_PER_STEP
    tok = pl.BlockSpec((nseq, chunk, D_A), lambda b, i: (b, i, 0))
    st = pl.BlockSpec((nseq, H_A, HD_A, HD_A), lambda b, i: (b, 0, 0, 0))
    full = lambda shape: pl.BlockSpec(shape, lambda b, i: (0,) * len(shape))
    tri = jnp.asarray(_tri_incl(chunk), BF16)
    ones = jnp.asarray(_block_ones(D_A, HD_A), BF16)
    return pl.pallas_call(
        _rwkv_body,
        grid=(bsz // nseq, nt),
        in_specs=[tok] * 7 + [st, full((chunk, chunk)), full((D_A, D_A)), full((1, D_A)), full((1, D_A)),
                  full((1, D_A))],
        out_specs=[tok, st],
        out_shape=[jax.ShapeDtypeStruct((bsz, t, D_A), F32),
                   jax.ShapeDtypeStruct((bsz, H_A, HD_A, HD_A), F32)],
        scratch_shapes=[pltpu.VMEM((nseq, H_A, HD_A, HD_A), F32)],
        compiler_params=_cparams("arbitrary", "arbitrary"),
        name="rwkv_scan_c%d" % chunk,
    )(r, lw, k, v, aa, bb, ga, s0, tri, ones, gn_g.reshape(1, -1), gn_b.reshape(1, -1), r_k.reshape(1, -1))


S5_LANE_BLOCKS = D_B // LANES
S5_STATES_PER_BLOCK = N_STATE_B // S5_LANE_BLOCKS
S5_ROWS = 8


def _s5_body(decode, u_ref, gb_ref, h0r_ref, h0i_ref, ar_ref, ai_ref, bbr_ref, bbi_ref, cr_ref, ci_ref, d_ref,
             gw_ref, gbias_ref, ob_o, hr_o, hi_o, car_r, car_i):
    u = u_ref[0]
    tb = u.shape[0]
    if not decode:
        @pl.when(pl.program_id(1) == 0)
        def _():
            car_r[...] = h0r_ref[0]
            car_i[...] = h0i_ref[0]
        sub = lax.broadcasted_iota(jnp.int32, (tb // S5_ROWS, S5_ROWS, S5_STATES_PER_BLOCK), 1)
    ys = []
    for j in range(S5_LANE_BLOCKS):
        sl = slice(S5_STATES_PER_BLOCK * j, S5_STATES_PER_BLOCK * (j + 1))
        uj = u[:, LANES * j:LANES * (j + 1)]
        bur = _mmw(uj, bbr_ref[j])
        bui = _mmw(uj, bbi_ref[j])
        pw_r = ar_ref[:, sl]
        pw_i = ai_ref[:, sl]
        ar, ai = pw_r[0:1, :], pw_i[0:1, :]
        if decode:
            h0r = h0r_ref[0][:, sl]
            h0i = h0i_ref[0][:, sl]
            hr = ar * h0r - ai * h0i + bur
            hi = ar * h0i + ai * h0r + bui
            hr_o[0, :, sl] = hr
            hi_o[0, :, sl] = hi
        else:
            width = S5_STATES_PER_BLOCK
            groups = tb // S5_ROWS
            hr3 = bur.reshape(groups, S5_ROWS, width)
            hi3 = bui.reshape(groups, S5_ROWS, width)
            pr, pi = ar.reshape(1, 1, width), ai.reshape(1, 1, width)
            s = 1
            while s < S5_ROWS:
                keep = sub >= s
                sr = jnp.where(keep, pltpu.roll(hr3, s, 1), 0.0)
                si = jnp.where(keep, pltpu.roll(hi3, s, 1), 0.0)
                hr3, hi3 = hr3 + pr * sr - pi * si, hi3 + pr * si + pi * sr
                pr, pi = pr * pr - pi * pi, 2.0 * pr * pi
                s *= 2
            cr = car_r[:, sl]
            ci = car_i[:, sl]
            rows_r, rows_i = [], []
            for g in range(groups):
                gr = hr3[g] + pw_r * cr - pw_i * ci
                gi = hi3[g] + pw_r * ci + pw_i * cr
                cr, ci = gr[S5_ROWS - 1:S5_ROWS, :], gi[S5_ROWS - 1:S5_ROWS, :]
                rows_r.append(gr)
                rows_i.append(gi)
            hr = jnp.concatenate(rows_r, axis=0)
            hi = jnp.concatenate(rows_i, axis=0)
            car_r[:, sl] = cr
            car_i[:, sl] = ci
            hr_o[0, :, sl] = cr
            hi_o[0, :, sl] = ci
        ys.append(_mm(hr, cr_ref[j]) - _mm(hi, ci_ref[j]))
    y = jnp.concatenate(ys, axis=-1) + d_ref[...] * u
    y = _gelu_tanh(y)
    y = y * _sigmoid(_mm(y, gw_ref[...]) + gbias_ref[...])
    ob_o[0] = y * _silu(gb_ref[0])


def _s5_mixer(u, gate_b, h0r, h0i, abar_re, abar_im, bbr, bbi, cr, ci, d_skip, glu_w, glu_b, decode):
    bsz, t, _ = u.shape
    tb = t if decode else min(S5_BLOCK, t)
    nt = t // tb
    st_rows = tb if decode else 1
    tok = pl.BlockSpec((1, tb, D_B), lambda b, i: (b, i, 0))
    st = pl.BlockSpec((1, st_rows, N_STATE_B), lambda b, i: (b, 0, 0))
    full = lambda shape: pl.BlockSpec(shape, lambda b, i: (0,) * len(shape))
    return pl.pallas_call(
        functools.partial(_s5_body, decode),
        grid=(bsz, nt),
        in_specs=[tok, tok, st, st, full((S5_ROWS, N_STATE_B)), full((S5_ROWS, N_STATE_B)),
                  full(bbr.shape), full(bbi.shape), full(cr.shape), full(ci.shape),
                  full((1, D_B)), full((D_B, D_B)), full((1, D_B))],
        out_specs=[tok, st, st],
        out_shape=[jax.ShapeDtypeStruct((bsz, t, D_B), F32),
                   jax.ShapeDtypeStruct((bsz, st_rows, N_STATE_B), F32),
                   jax.ShapeDtypeStruct((bsz, st_rows, N_STATE_B), F32)],
        scratch_shapes=[pltpu.VMEM((1, N_STATE_B), F32), pltpu.VMEM((1, N_STATE_B), F32)],
        compiler_params=_cparams("arbitrary", "arbitrary"),
        name="s5_decode" if decode else "s5_scan",
    )(u, gate_b, h0r, h0i, abar_re, abar_im, bbr, bbi, cr, ci, d_skip.reshape(1, -1), glu_w,
      glu_b.reshape(1, -1))


def _s5_params(lam_re, lam_im, log_dt, b_re, b_im, c_re, c_im):
    dt = jnp.exp(log_dt)[:, None]
    mag = jnp.exp(lam_re * dt)
    abar_re, abar_im = mag * jnp.cos(lam_im * dt), mag * jnp.sin(lam_im * dt)
    den = lam_re * lam_re + lam_im * lam_im
    nr, ni = abar_re - 1.0, abar_im
    f_re, f_im = (nr * lam_re + ni * lam_im) / den, (ni * lam_re - nr * lam_im) / den
    bb_re = f_re[..., None] * b_re - f_im[..., None] * b_im
    bb_im = f_re[..., None] * b_im + f_im[..., None] * b_re
    gpb = G_B // S5_LANE_BLOCKS
    eye = jnp.eye(gpb, dtype=F32)

    def stage_b(m):
        m = m.reshape(S5_LANE_BLOCKS, gpb, P_B, S5_GROUP)
        blk = jnp.einsum('jgpc,gh->jgchp', m, eye)
        return blk.reshape(S5_LANE_BLOCKS, gpb * S5_GROUP, gpb * P_B)

    def stage_c(m):
        m = m.reshape(S5_LANE_BLOCKS, gpb, S5_GROUP, P_B)
        blk = jnp.einsum('jgcp,gh->jgphc', m, eye)
        return blk.reshape(S5_LANE_BLOCKS, gpb * P_B, gpb * S5_GROUP)

    pw_re, pw_im = [abar_re.reshape(1, -1)], [abar_im.reshape(1, -1)]
    for _ in range(S5_ROWS - 1):
        pr, pi = pw_re[-1], pw_im[-1]
        pw_re.append(pr * pw_re[0] - pi * pw_im[0])
        pw_im.append(pr * pw_im[0] + pi * pw_re[0])
    return (jnp.concatenate(pw_re, axis=0), jnp.concatenate(pw_im, axis=0), stage_b(bb_re), stage_b(bb_im),
            stage_c(c_re), stage_c(c_im))


def _out_norm_body(a_ref, b_ref, x_ref, w_ref, g_ref, beta_ref, o_ref):
    half = a_ref.shape[-1]
    out = _mmw(a_ref[0], w_ref[0:half, :]) + _mmw(b_ref[0], w_ref[half:2 * half, :])
    h = DEEPNORM_ALPHA * x_ref[0] + out
    mu = jnp.mean(h, axis=-1, keepdims=True)
    var = jnp.mean(jnp.square(h - mu), axis=-1, keepdims=True)
    o_ref[0] = (h - mu) * lax.rsqrt(var + LN_EPS) * g_ref[...] + beta_ref[...]


def _out_norm(a, b, x, w_bf, g, beta):
    bsz, t, half = a.shape
    tb = min(PROJ_BLOCK, t)
    nt = t // tb
    tok = lambda n: pl.BlockSpec((1, tb, n), lambda bi, i: (bi, i, 0))
    full = lambda shape: pl.BlockSpec(shape, lambda bi, i: (0,) * len(shape))
    return pl.pallas_call(
        _out_norm_body,
        grid=(bsz, nt),
        in_specs=[tok(half), tok(half), tok(D_MODEL), full((2 * half, D_MODEL)), full((1, D_MODEL)),
                  full((1, D_MODEL))],
        out_specs=tok(D_MODEL),
        out_shape=jax.ShapeDtypeStruct((bsz, t, D_MODEL), F32),
        compiler_params=_cparams("arbitrary", "arbitrary"),
        name="out_norm",
    )(a, b, x, w_bf, g.reshape(1, -1), beta.reshape(1, -1))


O_QC, O_KC, O_VC, O_GC = 0, 512, 1024, 1536
O_QD, O_KD, O_VD, O_GD, O_AD = 2048, 2304, 2560, 3072, 3584
N_PROJ_O_STAGED = 3712


def _odd_prep_body(x_ref, w_ref, aw_ref, ab_ref, qc_o, kc_o, vc_o, kcb_o, vcb_o, gc_o, qd_o, kd_o, vd_o, la_o,
                   gd_o):
    x = x_ref[0].astype(BF16)
    proj = jnp.dot(x, w_ref[...], preferred_element_type=F32)
    kc = proj[:, O_KC:O_VC]
    vc = proj[:, O_VC:O_GC]
    qc_o[0] = (proj[:, O_QC:O_KC] * (HD_C ** -0.5)).astype(BF16)
    kc_o[0] = kc
    vc_o[0] = vc
    kcb_o[0] = kc.astype(BF16)
    vcb_o[0] = vc.astype(BF16)
    gc_o[0] = proj[:, O_GC:O_QD]
    qd_o[0] = proj[:, O_QD:O_KD] * (DK_D ** -0.5)
    kd_o[0] = proj[:, O_KD:O_VD]
    vd_o[0] = proj[:, O_VD:O_GD]
    gd_o[0] = proj[:, O_GD:O_AD]
    pre = _mm(proj[:, O_AD:N_PROJ_O_STAGED], aw_ref[...]) + ab_ref[...]
    la_o[0] = -_softplus(-pre) * (1.0 / GLA_TAU)


def _odd_prep(x, w_bf, alpha_w_pad, alpha_b):
    bsz, t, _ = x.shape
    tb = min(PROJ_BLOCK, t)
    nt = t // tb
    tok = lambda n: pl.BlockSpec((1, tb, n), lambda b, i: (b, i, 0))
    full = lambda shape: pl.BlockSpec(shape, lambda b, i: (0,) * len(shape))
    hk = H_D * DK_D
    widths = [(D_C, BF16), (D_C, F32), (D_C, F32), (D_C, BF16), (D_C, BF16), (D_C, F32),
              (hk, F32), (hk, F32), (D_D, F32), (hk, F32), (D_D, F32)]
    return pl.pallas_call(
        _odd_prep_body,
        grid=(bsz, nt),
        in_specs=[tok(D_MODEL), full((D_MODEL, N_PROJ_O_STAGED)), full((LANES, hk)), full((1, hk))],
        out_specs=[tok(n) for n, _ in widths],
        out_shape=[jax.ShapeDtypeStruct((bsz, t, n), dt) for n, dt in widths],
        compiler_params=_cparams("arbitrary", "arbitrary"),
        name="odd_prep",
    )(x, w_bf, alpha_w_pad, alpha_b.reshape(1, -1))


def _sb_prompt_body(q_ref, k_ref, v_ref, g_ref, bias_ref, ui_ref, o_ref):
    i = pl.program_id(2)
    q = q_ref[0]
    tq, width = q.shape
    heads = range(width // HD_C)
    pair = lambda h: slice(LANES * (h // 2), LANES * (h // 2 + 1))
    first = lax.broadcasted_iota(jnp.int32, (tq, LANES), 1) < HD_C
    zero = jnp.zeros((tq, LANES), q.dtype)
    qs = [jnp.where(first, q[:, pair(h)], zero) if h % 2 == 0 else jnp.where(first, zero, q[:, pair(h)])
          for h in heads]
    bias = bias_ref[0]
    biases = [bias[:, HD_C * h:HD_C * h + 1] for h in heads]
    ui = ui_ref[...]
    row = lax.broadcasted_iota(jnp.int32, (tq, tq), 0)
    col = lax.broadcasted_iota(jnp.int32, (tq, tq), 1)
    causal = col < row

    def tile(j, carry, masked):
        start = pl.multiple_of(j * tq, tq)
        kblk = k_ref[0, pl.ds(start, tq), :]
        vblk = v_ref[0, pl.ds(start, tq), :]
        zs = [lax.dot_general(qs[h], kblk[:, pair(h)], (((1,), (1,)), ((), ())), preferred_element_type=F32)
              + biases[h] for h in heads]
        incls = []
        for z in zs:
            sp = _softplus(z)
            if masked:
                sp = jnp.where(causal, sp, 0.0)
            incls.append(jnp.dot(sp.astype(BF16), ui, preferred_element_type=F32))
        new = []
        for h in heads:
            e = jnp.exp(zs[h] - incls[h] - carry[h][1])
            if masked:
                e = jnp.where(causal, e, 0.0)
            new.append((carry[h][0] + jnp.dot(e.astype(BF16), vblk[:, pair(h)], preferred_element_type=F32),
                        carry[h][1] + incls[h][:, 0:1]))
        return tuple(new)

    init = tuple((jnp.zeros((tq, LANES), F32), jnp.zeros((tq, 1), F32)) for _ in heads)
    carry = tile(i, init, True)
    carry = lax.fori_loop(0, i, lambda jj, c: tile(i - 1 - jj, c, False), carry)
    o = jnp.concatenate([jnp.where(first, carry[h][0], carry[h + 1][0]) for h in heads[::2]], axis=-1)
    o_ref[0] = o * _silu(g_ref[0])


def _sb_prompt(q_bf, k_bf, v_bf, g_c, sb_bias):
    bsz, t, _ = q_bf.shape
    tq = min(SB_TILE, t)
    nq = t // tq
    width = SB_HEADS * HD_C
    groups = D_C // width
    bias_lanes = jnp.repeat(sb_bias.astype(F32), HD_C).reshape(groups, 1, width)
    ui = jnp.asarray(np.tril(np.ones((tq, tq), np.float32)), BF16)
    qspec = pl.BlockSpec((1, tq, width), lambda b, p, i: (b, i, p))
    kvspec = pl.BlockSpec((1, t, width), lambda b, p, i: (b, 0, p))
    return pl.pallas_call(
        _sb_prompt_body,
        grid=(bsz, groups, nq),
        in_specs=[qspec, kvspec, kvspec, qspec,
                  pl.BlockSpec((1, 1, width), lambda b, p, i: (p, 0, 0)),
                  pl.BlockSpec((tq, tq), lambda b, p, i: (0, 0))],
        out_specs=qspec,
        out_shape=jax.ShapeDtypeStruct((bsz, t, D_C), F32),
        compiler_params=_cparams("arbitrary", "arbitrary", "arbitrary"),
        name="sb_prompt",
    )(q_bf, k_bf, v_bf, g_c, bias_lanes, ui)


def _sb_paged_body(pt_ref, qb_ref, g_ref, bias_ref, tri_ref, *refs):
    npg = PAGES_PER_STEP
    k_refs = refs[:npg]
    v_refs = refs[npg:2 * npg]
    o_ref = refs[2 * npg]
    acc, run = refs[2 * npg + 1:]
    j = pl.program_id(1)

    @pl.when(j == 0)
    def _():
        acc[...] = jnp.zeros_like(acc)
        run[...] = jnp.zeros_like(run)

    bias = bias_ref[:, 0:1]
    heads = range(H_C)
    for p in range(npg):
        z = jnp.concatenate([jnp.sum(k_refs[p][0, h] * qb_ref[0, h], axis=0, keepdims=True) for h in heads],
                            axis=0) + bias
        incl = _mm_sel(_softplus(z), tri_ref[...], 2)
        e = jnp.exp(z - incl - run[...])
        for h in heads:
            acc[h] += v_refs[p][0, h] * e[h:h + 1, :]
        run[...] += incl[:, 0:1]

    @pl.when(j == pl.num_programs(1) - 1)
    def _():
        ones = jnp.ones((8, PAGE), BF16)
        rows = []
        for h in heads:
            parts = [lax.dot_general(ones, part, (((1,), (1,)), ((), ())), preferred_element_type=F32)
                     for part in _split(acc[h], 3)]
            rows.append((parts[0] + parts[1] + parts[2])[0:1, :])
        o_ref[0] = jnp.concatenate(rows, axis=0) * _silu(g_ref[0])


def _sb_paged(q_s, g_s, cache_kt, cache_vt, page_table, sb_bias):
    bsz, n_pages = page_table.shape
    npg = PAGES_PER_STEP
    steps = n_pages // npg
    tri = jnp.asarray(np.tril(np.ones((PAGE, PAGE), np.float32)), BF16)
    bias = jnp.broadcast_to(sb_bias.astype(F32)[:, None], (H_C, LANES))
    qb = jnp.broadcast_to(q_s.astype(F32)[..., None], (bsz, H_C, HD_C, PAGE))

    def page_spec(p):
        return pl.BlockSpec((1, H_C, HD_C, PAGE),
                            lambda b, j, pt: (pt[b, n_pages - 1 - (j * npg + p)], 0, 0, 0))

    per_seq = pl.BlockSpec((1, H_C, HD_C), lambda b, j, pt: (b, 0, 0))
    full = lambda shape: pl.BlockSpec(shape, lambda b, j, pt: (0,) * len(shape))
    grid_spec = pltpu.PrefetchScalarGridSpec(
        num_scalar_prefetch=1,
        grid=(bsz, steps),
        in_specs=[pl.BlockSpec((1, H_C, HD_C, PAGE), lambda b, j, pt: (b, 0, 0, 0)), per_seq,
                  full((H_C, LANES)), full((PAGE, PAGE))] + [page_spec(p) for p in range(npg)] * 2,
        out_specs=per_seq,
        scratch_shapes=[pltpu.VMEM((H_C, HD_C, PAGE), F32), pltpu.VMEM((H_C, 1), F32)],
    )
    return pl.pallas_call(
        _sb_paged_body,
        grid_spec=grid_spec,
        out_shape=jax.ShapeDtypeStruct((bsz, H_C, HD_C), F32),
        compiler_params=_cparams("arbitrary", "arbitrary"),
        name="sb_paged",
    )(page_table, qb, g_s, bias, tri, *([cache_kt] * npg), *([cache_vt] * npg))


def _gla_body(q_ref, k_ref, v_ref, la_ref, gd_ref, s0_ref, dmat_ref, gng_ref, o_ref, sf_o, state):
    t_idx = pl.program_id(1)

    @pl.when(t_idx == 0)
    def _():
        state[...] = s0_ref[...]

    nseq, n, _ = q_ref.shape
    row = lax.broadcasted_iota(jnp.int32, (n, n), 0)
    col = lax.broadcasted_iota(jnp.int32, (n, n), 1)
    rsub = lax.broadcasted_iota(jnp.int32, (n, H_D * DK_D), 0)
    levels = _gla_levels(n)
    same_block = [(row >> int(math.log2(2 * m))) == (col >> int(math.log2(2 * m))) for m in levels]
    sls = [slice(DK_D * h, DK_D * (h + 1)) for h in range(H_D)]
    cast = lambda x: [x[:, sl].astype(BF16) for sl in sls]

    q_h, k_h, q_in, k_out, e_last, vh = [], [], [], [], [], []
    q_lv = [[] for _ in levels]
    k_lv = [[] for _ in levels]
    for s in range(nseq):
        q = q_ref[s]
        k = k_ref[s]
        dec = jnp.exp(_sel_mm(dmat_ref[...], la_ref[s], 3))
        blk = lambda idx: dec[n * idx:n * (idx + 1), :]
        q_h += cast(q)
        k_h += cast(k)
        q_in += cast(q * blk(0))
        k_out += cast(k * blk(1))
        e_last += [blk(0)[n - 1:n, sl] for sl in sls]
        for li, m in enumerate(levels):
            upper = (rsub & (2 * m - 1)) >= m
            e = blk(2 + li)
            q_lv[li] += cast(jnp.where(upper, q * e, 0.0))
            k_lv[li] += cast(jnp.where(upper, 0.0, k * e))
        v_all = v_ref[s]
        vh += [v_all[:, DV_D * h:DV_D * (h + 1)].astype(BF16) for h in range(H_D)]
    chains = range(nseq * H_D)
    st = [state[c // H_D, c % H_D] for c in chains]
    inter = [_mm_nt(q_in[c], st[c]) for c in chains]
    scores = [jnp.where(row == col, _mm_nt(q_h[c], k_h[c]), 0.0) for c in chains]
    for li in range(len(levels)):
        part = [_mm_nt(q_lv[li][c], k_lv[li][c]) for c in chains]
        scores = [scores[c] + jnp.where(same_block[li], part[c], 0.0) for c in chains]
    o = [inter[c] + _mm(scores[c], vh[c]) for c in chains]
    for c in chains:
        state[c // H_D, c % H_D] = st[c] * e_last[c] + _mm_tn(vh[c], k_out[c])
    outs = [oc * lax.rsqrt(jnp.mean(oc * oc, axis=-1, keepdims=True) + RMS_EPS) for oc in o]
    for s in range(nseq):
        o_ref[s] = jnp.concatenate(outs[s * H_D:(s + 1) * H_D], axis=-1) * gng_ref[...] * _silu(gd_ref[s])

    @pl.when(t_idx == pl.num_programs(1) - 1)
    def _():
        sf_o[...] = state[...]


def _gla_scan(q, k, v, la, gd, s0_t, gn_g, chunk):
    bsz, t, hk = q.shape
    nt = t // chunk
    nseq = GLA_SEQS_PER_STEP
    tok = lambda n: pl.BlockSpec((nseq, chunk, n), lambda b, i: (b, i, 0))
    st = pl.BlockSpec((nseq, H_D, DV_D, DK_D), lambda b, i: (b, 0, 0, 0))
    full = lambda shape: pl.BlockSpec(shape, lambda b, i: (0,) * len(shape))
    dmat = jnp.asarray(_gla_decay_rows(chunk), BF16)
    return pl.pallas_call(
        _gla_body,
        grid=(bsz // nseq, nt),
        in_specs=[tok(hk), tok(hk), tok(D_D), tok(hk), tok(D_D), st, full(dmat.shape), full((1, D_D))],
        out_specs=[tok(D_D), st],
        out_shape=[jax.ShapeDtypeStruct((bsz, t, D_D), F32),
                   jax.ShapeDtypeStruct((bsz, H_D, DV_D, DK_D), F32)],
        scratch_shapes=[pltpu.VMEM((nseq, H_D, DV_D, DK_D), F32)],
        compiler_params=_cparams("arbitrary", "arbitrary"),
        name="gla_scan_c%d" % chunk,
    )(q, k, v, la, gd, s0_t, dmat, gn_g.reshape(1, -1))


def _pad_tokens(a, n):
    return jnp.pad(a, ((0, 0), (0, n - a.shape[1]), (0, 0)))


def kernel(x_prompt, x_sample, state_rwkv, state_shift, state_s5_re, state_s5_im, cache_k, cache_v, state_gla,
           page_table, ln_g, ln_b, w_in_e, mu_shift, rwkv_w0, rwkv_w2, rwkv_a0, rwkv_a2, rwkv_k_k, rwkv_k_a,
           rwkv_r_k, rwkv_gn_g, rwkv_gn_b, s5_lambda_re, s5_lambda_im, s5_log_dt, s5_b_re, s5_b_im, s5_c_re,
           s5_c_im, s5_d, s5_glu_w, s5_glu_b, w_out_e, w_in_o, sb_bias, gla_alpha_w, gla_alpha_b, gla_gn_g,
           w_out_o):
    bp, t_p, _ = x_prompt.shape
    bs = x_sample.shape[0]
    n_phys = cache_k.shape[1]
    xs_rows = x_sample.reshape(1, bs, D_MODEL)

    w_in_bf = w_in_e[0].astype(BF16)
    prep_w = (w_in_bf, mu_shift[0], rwkv_w0[0], rwkv_w2[0], rwkv_a0[0], rwkv_a2[0], rwkv_k_k[0], rwkv_k_a[0])
    pe = _even_prep(x_prompt, jnp.zeros((bp, 1, N_SHIFT), F32), *prep_w, decode=False)
    se = _even_prep(xs_rows, state_shift[0].reshape(1, bs, N_SHIFT), *prep_w, decode=True)
    rw_w = (rwkv_gn_g[0], rwkv_gn_b[0], rwkv_r_k[0])
    oa_p, rw_p = _rwkv_scan(*pe[:7], jnp.zeros((bp, H_A, HD_A, HD_A), F32), *rw_w, chunk=RWKV_CHUNK)
    se_tok = [_pad_tokens(a.reshape(bs, 1, D_A), DECODE_CHUNK) for a in se[:7]]
    oa_s, rw_s = _rwkv_scan(*se_tok, state_rwkv[0], *rw_w, chunk=DECODE_CHUNK)

    abar_re, abar_im, bbr, bbi, cr, ci = _s5_params(s5_lambda_re[0], s5_lambda_im[0], s5_log_dt[0], s5_b_re[0],
                                                    s5_b_im[0], s5_c_re[0], s5_c_im[0])
    s5_tail = (cr.astype(BF16), ci.astype(BF16), s5_d[0], s5_glu_w[0].astype(BF16), s5_glu_b[0])
    zst = jnp.zeros((bp, 1, N_STATE_B), F32)
    ob_p, re_p, im_p = _s5_mixer(pe[7], pe[8], zst, zst, abar_re, abar_im, bbr.astype(BF16), bbi.astype(BF16),
                                 *s5_tail, decode=False)
    ob_s, re_s, im_s = _s5_mixer(se[7], se[8], state_s5_re[0].reshape(1, bs, N_STATE_B),
                                 state_s5_im[0].reshape(1, bs, N_STATE_B), abar_re, abar_im, bbr, bbi,
                                 *s5_tail, decode=True)

    w_out_e_bf = w_out_e[0].astype(BF16)
    x1_p = _out_norm(oa_p, ob_p, x_prompt, w_out_e_bf, ln_g[0], ln_b[0])
    x1_s = _out_norm(oa_s[:, 0:1, :].reshape(1, bs, D_A), ob_s, xs_rows, w_out_e_bf, ln_g[0], ln_b[0])

    w = w_in_o[0]
    w_o_bf = jnp.concatenate([w[:, :O_GD], w[:, O_GD + R_G:], w[:, O_GD:O_GD + R_G],
                              jnp.zeros((D_MODEL, N_PROJ_O_STAGED - w.shape[1]), F32)], axis=1).astype(BF16)
    alpha_w_pad = jnp.pad(gla_alpha_w[0], ((0, LANES - R_G), (0, 0)))
    po = _odd_prep(x1_p, w_o_bf, alpha_w_pad, gla_alpha_b[0])
    so = _odd_prep(x1_s, w_o_bf, alpha_w_pad, gla_alpha_b[0])
    qc_p, kc_p, vc_p, kcb_p, vcb_p, gc_p, qd_p, kd_p, vd_p, la_p, gd_p = po
    qc_s, kc_s, vc_s, _, _, gc_s, qd_s, kd_s, vd_s, la_s, gd_s = so

    oc_p = _sb_prompt(qc_p, kcb_p, vcb_p, gc_p, sb_bias[0])
    oc_s = _sb_paged(qc_s.reshape(bs, H_C, HD_C), gc_s.reshape(bs, H_C, HD_C),
                     jnp.transpose(cache_k[0], (0, 2, 3, 1)), jnp.transpose(cache_v[0], (0, 2, 3, 1)),
                     page_table, sb_bias[0])

    od_p, gl_p = _gla_scan(qd_p, kd_p, vd_p, la_p, gd_p, jnp.zeros((bp, H_D, DV_D, DK_D), F32), gla_gn_g[0],
                           chunk=GLA_CHUNK)
    tok_s = [_pad_tokens(a.reshape(bs, 1, a.shape[-1]), DECODE_CHUNK) for a in (qd_s, kd_s, vd_s, la_s, gd_s)]
    od_s, gl_s = _gla_scan(*tok_s, jnp.swapaxes(state_gla[0], -1, -2), gla_gn_g[0], chunk=DECODE_CHUNK)

    w_out_o_bf = w_out_o[0].astype(BF16)
    y_p = _out_norm(oc_p, od_p, x1_p, w_out_o_bf, ln_g[1], ln_b[1])
    y_s = _out_norm(oc_s.reshape(1, bs, D_C), od_s[:, 0:1, :].reshape(1, bs, D_D), x1_s, w_out_o_bf, ln_g[1],
                    ln_b[1])

    heads_c = lambda a, b, t: a.reshape(1, b, t, H_C, HD_C)
    return (y_p, y_s.reshape(bs, 1, D_MODEL),
            rw_p[None], rw_s[None],
            pe[9].reshape(1, bp, N_SHIFT), se[9].reshape(1, bs, N_SHIFT),
            re_p.reshape(1, bp, G_B, P_B), re_s.reshape(1, bs, G_B, P_B),
            im_p.reshape(1, bp, G_B, P_B), im_s.reshape(1, bs, G_B, P_B),
            heads_c(kc_p, bp, t_p), heads_c(kc_s, bs, 1), heads_c(vc_p, bp, t_p), heads_c(vc_s, bs, 1),
            jnp.swapaxes(gl_p, -1, -2)[None], jnp.swapaxes(gl_s, -1, -2)[None])
```

```python
import functools
import math

import numpy as np
import jax
import jax.numpy as jnp
from jax import lax
from jax.experimental import pallas as pl
from jax.experimental.pallas import tpu as pltpu

F32 = jnp.float32
BF16 = jnp.bfloat16

D_MODEL = 1024
D_A = 512
HD_A = 64
H_A = 8
R_W = 64
R_A = 64
D_B = 512
S5_GROUP = 16
G_B = 32
P_B = 64
N_STATE_B = G_B * P_B
D_C = 512
HD_C = 64
H_C = 8
D_D = 512
H_D = 4
DV_D = 128
DK_D = 64
R_G = 16
GLA_TAU = 16.0
N_SHIFT = 3 * D_A + R_W + R_A
N_PROJ_E = N_SHIFT + D_A + 2 * D_B
DEPTH = 2
DEEPNORM_ALPHA = (2.0 * DEPTH) ** 0.25
LN_EPS = 1e-5
GN_EPS = 64e-5
RMS_EPS = 1e-5
PAGE = 128

LOG2E = 1.4426950408889634
LANES = 128
VMEM_LIMIT = 56 * 1024 * 1024

PROJ_BLOCK = 512
OUT_BLOCK = 1024
S5_BLOCK = 256
RWKV_CHUNK = 64
RWKV_SEQS_PER_STEP = 4
GLA_CHUNK = 64
GLA_SEQS_PER_STEP = 4
DECODE_CHUNK = 8
SB_TILE = 256
SB_HEADS = 8
PAGES_PER_STEP = 16


def _cparams(*sem):
    return pltpu.CompilerParams(dimension_semantics=sem, vmem_limit_bytes=VMEM_LIMIT)


def _mm(a, b):
    return jnp.dot(a.astype(BF16), b.astype(BF16), preferred_element_type=F32)


def _mm_nt(a, b):
    return lax.dot_general(a.astype(BF16), b.astype(BF16), (((1,), (1,)), ((), ())),
                           preferred_element_type=F32)


def _mm_tn(a, b):
    return lax.dot_general(a.astype(BF16), b.astype(BF16), (((0,), (0,)), ((), ())),
                           preferred_element_type=F32)


def _mm_f32(a, b):
    return jnp.dot(a, b, precision=lax.Precision.HIGHEST, preferred_element_type=F32)


def _mmw(a, w):
    if w.dtype == BF16:
        return jnp.dot(a.astype(BF16), w, preferred_element_type=F32)
    return _mm_f32(a, w)


def _split(x, parts):
    out = []
    rem = x
    for _ in range(parts - 1):
        p = rem.astype(BF16)
        out.append(p)
        rem = rem - p.astype(F32)
    out.append(rem.astype(BF16))
    return out


def _sel_mm(sel, x, parts):
    acc = None
    for p in _split(x, parts):
        t = jnp.dot(sel, p, preferred_element_type=F32)
        acc = t if acc is None else acc + t
    return acc


def _mm_sel(x, sel, parts):
    acc = None
    for p in _split(x, parts):
        t = jnp.dot(p, sel, preferred_element_type=F32)
        acc = t if acc is None else acc + t
    return acc


def _sigmoid(t):
    return 1.0 / (1.0 + jnp.exp(-t))


def _silu(t):
    return t * _sigmoid(t)


def _log1pexp_negabs(t):
    return jnp.log(1.0 + jnp.exp2(jnp.abs(t) * (-LOG2E)))


def _softplus(t):
    return jnp.maximum(t, 0.0) + _log1pexp_negabs(t)


def _gelu_tanh(x):
    return 0.5 * x * (1.0 + jnp.tanh(math.sqrt(2.0 / math.pi) * (x + 0.044715 * (x * x * x))))


def _tri_incl(n):
    return np.tril(np.ones((n, n), np.float32))


def _block_ones(n, blk):
    idx = np.arange(n) // blk
    return (idx[:, None] == idx[None, :]).astype(np.float32)


def _gla_decay_rows(n):
    t = np.arange(n)[:, None]
    i = np.arange(n)[None, :]
    mats = [(i <= t), (i > t)]
    m = n // 2
    while m >= 1:
        p = 2 * m * (t // (2 * m)) + m - 1
        upper = (t % (2 * m)) >= m
        mats.append(np.where(upper, (i > p) & (i <= t), (i > t) & (i <= p)))
        m //= 2
    return np.concatenate([x.astype(np.float32) for x in mats], axis=0)


def _gla_levels(n):
    out = []
    m = n // 2
    while m >= 1:
        out.append(m)
        m //= 2
    return out


def _even_prep_body(decode, x_ref, zp_ref, w_ref, mu_ref, w0_ref, w2_ref, a0_ref, a2_ref, kk_ref, ka_ref,
                    ones_ref, r_o, lw_o, k_o, v_o, aa_o, bb_o, ga_o, u_o, gb_o, sh_o, carry):
    x = x_ref[0].astype(BF16)
    proj = jnp.dot(x, w_ref[...], preferred_element_type=F32)
    z = proj[:, :N_SHIFT]
    tb = z.shape[0]
    if decode:
        zp = zp_ref[0]
        sh_o[0] = z
    else:
        @pl.when(pl.program_id(1) == 0)
        def _():
            carry[...] = zp_ref[0]
        row = lax.broadcasted_iota(jnp.int32, z.shape, 0)
        zp = jnp.where(row == 0, carry[...], pltpu.roll(z, 1, 0))
        carry[...] = z[tb - 1:tb, :]
        sh_o[0] = z[tb - 1:tb, :]
    zm = z + mu_ref[...] * (zp - z)
    r = zm[:, 0:D_A]
    k = zm[:, D_A:2 * D_A]
    v = zm[:, 2 * D_A:3 * D_A]
    wd = zm[:, 3 * D_A:3 * D_A + R_W]
    ad = zm[:, 3 * D_A + R_W:N_SHIFT]
    w = -_softplus(-(w0_ref[...] + _mm(jnp.tanh(wd), w2_ref[...]))) - 0.5
    a = _sigmoid(a0_ref[...] + _mm(ad, a2_ref[...]))
    kk = k * kk_ref[...]
    ss = _mm_sel(kk * kk, ones_ref[...], 2)
    kkn = kk / jnp.maximum(jnp.sqrt(ss), 1e-12)
    r_o[0] = r
    lw_o[0] = -jnp.exp(w)
    k_o[0] = k * (1.0 + (a - 1.0) * ka_ref[...])
    v_o[0] = v
    aa_o[0] = -kkn
    bb_o[0] = kkn * a
    ga_o[0] = proj[:, N_SHIFT:N_SHIFT + D_A]
    u_o[0] = proj[:, N_SHIFT + D_A:N_SHIFT + D_A + D_B]
    gb_o[0] = proj[:, N_SHIFT + D_A + D_B:N_PROJ_E]


def _even_prep(x, zprev, w_bf, mu, w0, w2, a0, a2, k_k, k_a, decode):
    bsz, t, _ = x.shape
    tb = t if decode else min(PROJ_BLOCK, t)
    nt = t // tb
    row2 = lambda a: a.reshape(1, -1)
    ones = jnp.asarray(_block_ones(D_A, HD_A), BF16)
    full = lambda shape: pl.BlockSpec(shape, lambda b, i: (0,) * len(shape))
    tok = lambda n: pl.BlockSpec((1, tb, n), lambda b, i: (b, i, 0))
    zp_spec = (pl.BlockSpec((1, tb, N_SHIFT), lambda b, i: (b, i, 0)) if decode
               else pl.BlockSpec((1, 1, N_SHIFT), lambda b, i: (b, 0, 0)))
    sh_rows = tb if decode else 1
    outs = [jax.ShapeDtypeStruct((bsz, t, D_A), F32)] * 9 + [jax.ShapeDtypeStruct((bsz, sh_rows, N_SHIFT), F32)]
    out_specs = [tok(D_A)] * 9 + [pl.BlockSpec((1, sh_rows, N_SHIFT), lambda b, i: (b, 0, 0))]
    return pl.pallas_call(
        functools.partial(_even_prep_body, decode),
        grid=(bsz, nt),
        in_specs=[tok(D_MODEL), zp_spec, full((D_MODEL, N_PROJ_E)), full((1, N_SHIFT)), full((1, D_A)),
                  full((R_W, D_A)), full((1, D_A)), full((R_A, D_A)), full((1, D_A)), full((1, D_A)),
                  full((D_A, D_A))],
        out_specs=out_specs,
        out_shape=outs,
        scratch_shapes=[pltpu.VMEM((1, N_SHIFT), F32)],
        compiler_params=_cparams("arbitrary", "arbitrary"),
        name="even_prep_decode" if decode else "even_prep",
    )(x, zprev, w_bf, row2(mu), row2(w0), w2, row2(a0), a2, row2(k_k), row2(k_a), ones)


def _rwkv_body(r_ref, lw_ref, k_ref, v_ref, aa_ref, bb_ref, ga_ref, s0_ref, tri_ref, ones_ref, gng_ref, gnb_ref,
               rk_ref, oa_o, sf_o, state):
    t_idx = pl.program_id(1)

    @pl.when(t_idx == 0)
    def _():
        state[...] = s0_ref[...]

    nseq, n, _ = lw_ref.shape
    row = lax.broadcasted_iota(jnp.int32, (n, n), 0)
    col = lax.broadcasted_iota(jnp.int32, (n, n), 1)
    strict = col < row
    incl = col <= row
    levels = int(math.log2(n))
    sls = [slice(HD_A * h, HD_A * (h + 1)) for h in range(H_A)]
    cast = lambda x: [x[:, sl].astype(BF16) for sl in sls]

    a_h, b_h, k_h, r_h, v_h, b_lh, k_lh, p_last_h, v_seq, rkr_seq = [], [], [], [], [], [], [], [], [], []
    for s in range(nseq):
        lw = lw_ref[s]
        c = _sel_mm(tri_ref[...], lw, 3)
        c_last = c[n - 1:n, :]
        r = r_ref[s]
        k = k_ref[s]
        v = v_ref[s]
        bb = bb_ref[s]
        pinv = jnp.exp(-c)
        tail = jnp.exp(c_last - c)
        p_last = jnp.exp(c_last)
        a_h += cast(aa_ref[s] * jnp.exp(c - lw))
        b_h += cast(bb * pinv)
        k_h += cast(k * pinv)
        r_h += cast(r * jnp.exp(c))
        v_h += cast(v)
        b_lh += cast(bb * tail)
        k_lh += cast(k * tail)
        p_last_h += [p_last[:, sl] for sl in sls]
        v_seq.append(v)
        rkr_seq.append(r * k * rk_ref[...])
    chains = range(nseq * H_A)
    s_h = [state[c // H_A, c % H_A] for c in chains]
    s_bf = [x.astype(BF16) for x in s_h]
    nmat = [jnp.where(strict, _mm_nt(a_h[c], b_h[c]), 0.0) for c in chains]
    mmat = [jnp.where(strict, _mm_nt(a_h[c], k_h[c]), 0.0) for c in chains]
    rbm = [jnp.where(incl, _mm_nt(r_h[c], b_h[c]), 0.0) for c in chains]
    rkm = [jnp.where(incl, _mm_nt(r_h[c], k_h[c]), 0.0) for c in chains]
    u = [_mm_nt(a_h[c], s_bf[c]) + _mm(mmat[c], v_h[c]) for c in chains]
    y0 = [_mm_nt(r_h[c], s_bf[c]) + _mm(rkm[c], v_h[c]) for c in chains]
    npow = nmat
    for lvl in range(levels):
        u = [u[c] + _mm(npow[c], u[c]) for c in chains]
        if lvl + 1 < levels:
            npow = [_mm(npow[c], npow[c]) for c in chains]
    y = [y0[c] + _mm(rbm[c], u[c]) for c in chains]
    for c in chains:
        state[c // H_A, c % H_A] = s_h[c] * p_last_h[c] + _mm_tn(u[c], b_lh[c]) + _mm_tn(v_h[c], k_lh[c])
    for s in range(nseq):
        yn = []
        for yh in y[s * H_A:(s + 1) * H_A]:
            mu = jnp.mean(yh, axis=-1, keepdims=True)
            var = jnp.mean(jnp.square(yh - mu), axis=-1, keepdims=True)
            yn.append((yh - mu) * lax.rsqrt(var + GN_EPS))
        yn = jnp.concatenate(yn, axis=-1)
        rk_head = _mm_sel(rkr_seq[s], ones_ref[...], 2)
        oa_o[s] = (yn * gng_ref[...] + gnb_ref[...] + rk_head * v_seq[s]) * _silu(ga_ref[s])

    @pl.when(t_idx == pl.num_programs(1) - 1)
    def _():
        sf_o[...] = state[...]


def _rwkv_scan(r, lw, k, v, aa, bb, ga, s0, gn_g, gn_b, r_k, chunk):
    bsz, t, _ = r.shape
    nt = t // chunk
    nseq = RWKV_SEQS_PER_STEP
    tok = pl.BlockSpec((nseq, chunk, D_A), lambda b, i: (b, i, 0))
    st = pl.BlockSpec((nseq, H_A, HD_A, HD_A), lambda b, i: (b, 0, 0, 0))
    full = lambda shape: pl.BlockSpec(shape, lambda b, i: (0,) * len(shape))
    tri = jnp.asarray(_tri_incl(chunk), BF16)
    ones = jnp.asarray(_block_ones(D_A, HD_A), BF16)
    return pl.pallas_call(
        _rwkv_body,
        grid=(bsz // nseq, nt),
        in_specs=[tok] * 7 + [st, full((chunk, chunk)), full((D_A, D_A)), full((1, D_A)), full((1, D_A)),
                  full((1, D_A))],
        out_specs=[tok, st],
        out_shape=[jax.ShapeDtypeStruct((bsz, t, D_A), F32),
                   jax.ShapeDtypeStruct((bsz, H_A, HD_A, HD_A), F32)],
        scratch_shapes=[pltpu.VMEM((nseq, H_A, HD_A, HD_A), F32)],
        compiler_params=_cparams("arbitrary", "arbitrary"),
        name="rwkv_scan_c%d" % chunk,
    )(r, lw, k, v, aa, bb, ga, s0, tri, ones, gn_g.reshape(1, -1), gn_b.reshape(1, -1), r_k.reshape(1, -1))


S5_LANE_BLOCKS = D_B // LANES
S5_STATES_PER_BLOCK = N_STATE_B // S5_LANE_BLOCKS
S5_ROWS = 8


def _s5_body(decode, u_ref, gb_ref, h0r_ref, h0i_ref, ar_ref, ai_ref, bbr_ref, bbi_ref, cr_ref, ci_ref, d_ref,
             gw_ref, gbias_ref, ob_o, hr_o, hi_o, car_r, car_i):
    u = u_ref[0]
    tb = u.shape[0]
    if not decode:
        @pl.when(pl.program_id(1) == 0)
        def _():
            car_r[...] = h0r_ref[0]
            car_i[...] = h0i_ref[0]
        sub = lax.broadcasted_iota(jnp.int32, (1, S5_ROWS, S5_STATES_PER_BLOCK), 1)
    ys = []
    for j in range(S5_LANE_BLOCKS):
        sl = slice(S5_STATES_PER_BLOCK * j, S5_STATES_PER_BLOCK * (j + 1))
        uj = u[:, LANES * j:LANES * (j + 1)]
        bur = _mmw(uj, bbr_ref[j])
        bui = _mmw(uj, bbi_ref[j])
        pw_r = ar_ref[:, sl]
        pw_i = ai_ref[:, sl]
        ar, ai = pw_r[0:1, :], pw_i[0:1, :]
        if decode:
            h0r = h0r_ref[0][:, sl]
            h0i = h0i_ref[0][:, sl]
            hr = ar * h0r - ai * h0i + bur
            hi = ar * h0i + ai * h0r + bui
            hr_o[0, :, sl] = hr
            hi_o[0, :, sl] = hi
        else:
            width = S5_STATES_PER_BLOCK
            groups = tb // S5_ROWS
            hr3 = bur.reshape(groups, S5_ROWS, width)
            hi3 = bui.reshape(groups, S5_ROWS, width)
            pr, pi = ar.reshape(1, 1, width), ai.reshape(1, 1, width)
            s = 1
            while s < S5_ROWS:
                keep = sub >= s
                mr = jnp.where(keep, pr, 0.0)
                mi = jnp.where(keep, pi, 0.0)
                sr = pltpu.roll(hr3, s, 1)
                si = pltpu.roll(hi3, s, 1)
                hr3, hi3 = hr3 + mr * sr - mi * si, hi3 + mr * si + mi * sr
                pr, pi = pr * pr - pi * pi, 2.0 * pr * pi
                s *= 2
            cr = car_r[:, sl]
            ci = car_i[:, sl]
            rows_r, rows_i = [], []
            for g in range(groups):
                gr = hr3[g] + pw_r * cr - pw_i * ci
                gi = hi3[g] + pw_r * ci + pw_i * cr
                cr, ci = gr[S5_ROWS - 1:S5_ROWS, :], gi[S5_ROWS - 1:S5_ROWS, :]
                rows_r.append(gr)
                rows_i.append(gi)
            hr = jnp.concatenate(rows_r, axis=0)
            hi = jnp.concatenate(rows_i, axis=0)
            car_r[:, sl] = cr
            car_i[:, sl] = ci
            hr_o[0, :, sl] = cr
            hi_o[0, :, sl] = ci
        ys.append(_mm(hr, cr_ref[j]) - _mm(hi, ci_ref[j]))
    y = jnp.concatenate(ys, axis=-1) + d_ref[...] * u
    y = _gelu_tanh(y)
    y = y * _sigmoid(_mm(y, gw_ref[...]) + gbias_ref[...])
    ob_o[0] = y * _silu(gb_ref[0])


def _s5_mixer(u, gate_b, h0r, h0i, abar_re, abar_im, bbr, bbi, cr, ci, d_skip, glu_w, glu_b, decode):
    bsz, t, _ = u.shape
    tb = t if decode else min(S5_BLOCK, t)
    nt = t // tb
    st_rows = tb if decode else 1
    tok = pl.BlockSpec((1, tb, D_B), lambda b, i: (b, i, 0))
    st = pl.BlockSpec((1, st_rows, N_STATE_B), lambda b, i: (b, 0, 0))
    full = lambda shape: pl.BlockSpec(shape, lambda b, i: (0,) * len(shape))
    return pl.pallas_call(
        functools.partial(_s5_body, decode),
        grid=(bsz, nt),
        in_specs=[tok, tok, st, st, full((S5_ROWS, N_STATE_B)), full((S5_ROWS, N_STATE_B)),
                  full(bbr.shape), full(bbi.shape), full(cr.shape), full(ci.shape),
                  full((1, D_B)), full((D_B, D_B)), full((1, D_B))],
        out_specs=[tok, st, st],
        out_shape=[jax.ShapeDtypeStruct((bsz, t, D_B), F32),
                   jax.ShapeDtypeStruct((bsz, st_rows, N_STATE_B), F32),
                   jax.ShapeDtypeStruct((bsz, st_rows, N_STATE_B), F32)],
        scratch_shapes=[pltpu.VMEM((1, N_STATE_B), F32), pltpu.VMEM((1, N_STATE_B), F32)],
        compiler_params=_cparams("arbitrary", "arbitrary"),
        name="s5_decode" if decode else "s5_scan",
    )(u, gate_b, h0r, h0i, abar_re, abar_im, bbr, bbi, cr, ci, d_skip.reshape(1, -1), glu_w,
      glu_b.reshape(1, -1))


def _s5_params(lam_re, lam_im, log_dt, b_re, b_im, c_re, c_im):
    dt = jnp.exp(log_dt)[:, None]
    mag = jnp.exp(lam_re * dt)
    abar_re, abar_im = mag * jnp.cos(lam_im * dt), mag * jnp.sin(lam_im * dt)
    den = lam_re * lam_re + lam_im * lam_im
    nr, ni = abar_re - 1.0, abar_im
    f_re, f_im = (nr * lam_re + ni * lam_im) / den, (ni * lam_re - nr * lam_im) / den
    bb_re = f_re[..., None] * b_re - f_im[..., None] * b_im
    bb_im = f_re[..., None] * b_im + f_im[..., None] * b_re
    gpb = G_B // S5_LANE_BLOCKS
    eye = jnp.eye(gpb, dtype=F32)

    def stage_b(m):
        m = m.reshape(S5_LANE_BLOCKS, gpb, P_B, S5_GROUP)
        blk = jnp.einsum('jgpc,gh->jgchp', m, eye)
        return blk.reshape(S5_LANE_BLOCKS, gpb * S5_GROUP, gpb * P_B)

    def stage_c(m):
        m = m.reshape(S5_LANE_BLOCKS, gpb, S5_GROUP, P_B)
        blk = jnp.einsum('jgcp,gh->jgphc', m, eye)
        return blk.reshape(S5_LANE_BLOCKS, gpb * P_B, gpb * S5_GROUP)

    pw_re, pw_im = [abar_re.reshape(1, -1)], [abar_im.reshape(1, -1)]
    for _ in range(S5_ROWS - 1):
        pr, pi = pw_re[-1], pw_im[-1]
        pw_re.append(pr * pw_re[0] - pi * pw_im[0])
        pw_im.append(pr * pw_im[0] + pi * pw_re[0])
    return (jnp.concatenate(pw_re, axis=0), jnp.concatenate(pw_im, axis=0), stage_b(bb_re), stage_b(bb_im),
            stage_c(c_re), stage_c(c_im))


def _out_norm_body(a_ref, b_ref, x_ref, w_ref, g_ref, beta_ref, o_ref):
    half = a_ref.shape[-1]
    out = _mmw(a_ref[0], w_ref[0:half, :]) + _mmw(b_ref[0], w_ref[half:2 * half, :])
    h = DEEPNORM_ALPHA * x_ref[0] + out
    mu = jnp.mean(h, axis=-1, keepdims=True)
    var = jnp.mean(jnp.square(h - mu), axis=-1, keepdims=True)
    o_ref[0] = (h - mu) * lax.rsqrt(var + LN_EPS) * g_ref[...] + beta_ref[...]


def _out_norm(a, b, x, w_bf, g, beta):
    bsz, t, half = a.shape
    tb = min(OUT_BLOCK, t)
    nt = t // tb
    tok = lambda n: pl.BlockSpec((1, tb, n), lambda bi, i: (bi, i, 0))
    full = lambda shape: pl.BlockSpec(shape, lambda bi, i: (0,) * len(shape))
    return pl.pallas_call(
        _out_norm_body,
        grid=(bsz, nt),
        in_specs=[tok(half), tok(half), tok(D_MODEL), full((2 * half, D_MODEL)), full((1, D_MODEL)),
                  full((1, D_MODEL))],
        out_specs=tok(D_MODEL),
        out_shape=jax.ShapeDtypeStruct((bsz, t, D_MODEL), F32),
        compiler_params=_cparams("arbitrary", "arbitrary"),
        name="out_norm",
    )(a, b, x, w_bf, g.reshape(1, -1), beta.reshape(1, -1))


O_QC, O_KC, O_VC, O_GC = 0, 512, 1024, 1536
O_QD, O_KD, O_VD, O_GD, O_AD = 2048, 2304, 2560, 3072, 3584
N_PROJ_O_STAGED = 3712


def _odd_prep_body(x_ref, w_ref, aw_ref, ab_ref, qc_o, kc_o, vc_o, kcb_o, vcb_o, gc_o, qd_o, kd_o, vd_o, la_o,
                   gd_o):
    x = x_ref[0].astype(BF16)
    proj = jnp.dot(x, w_ref[...], preferred_element_type=F32)
    kc = proj[:, O_KC:O_VC]
    vc = proj[:, O_VC:O_GC]
    qc_o[0] = (proj[:, O_QC:O_KC] * (HD_C ** -0.5)).astype(BF16)
    kc_o[0] = kc
    vc_o[0] = vc
    kcb_o[0] = kc.astype(BF16)
    vcb_o[0] = vc.astype(BF16)
    gc_o[0] = proj[:, O_GC:O_QD]
    qd_o[0] = proj[:, O_QD:O_KD] * (DK_D ** -0.5)
    kd_o[0] = proj[:, O_KD:O_VD]
    vd_o[0] = proj[:, O_VD:O_GD]
    gd_o[0] = proj[:, O_GD:O_AD]
    pre = _mm(proj[:, O_AD:N_PROJ_O_STAGED], aw_ref[...]) + ab_ref[...]
    la_o[0] = -_softplus(-pre) * (1.0 / GLA_TAU)


def _odd_prep(x, w_bf, alpha_w_pad, alpha_b):
    bsz, t, _ = x.shape
    tb = min(PROJ_BLOCK, t)
    nt = t // tb
    tok = lambda n: pl.BlockSpec((1, tb, n), lambda b, i: (b, i, 0))
    full = lambda shape: pl.BlockSpec(shape, lambda b, i: (0,) * len(shape))
    hk = H_D * DK_D
    widths = [(D_C, BF16), (D_C, F32), (D_C, F32), (D_C, BF16), (D_C, BF16), (D_C, F32),
              (hk, F32), (hk, F32), (D_D, F32), (hk, F32), (D_D, F32)]
    return pl.pallas_call(
        _odd_prep_body,
        grid=(bsz, nt),
        in_specs=[tok(D_MODEL), full((D_MODEL, N_PROJ_O_STAGED)), full((LANES, hk)), full((1, hk))],
        out_specs=[tok(n) for n, _ in widths],
        out_shape=[jax.ShapeDtypeStruct((bsz, t, n), dt) for n, dt in widths],
        compiler_params=_cparams("arbitrary", "arbitrary"),
        name="odd_prep",
    )(x, w_bf, alpha_w_pad, alpha_b.reshape(1, -1))


def _sb_prompt_body(q_ref, k_ref, v_ref, g_ref, bias_ref, ui_ref, o_ref, acc_ref, run_ref):
    i = pl.program_id(2)
    q = q_ref[0]
    tq, width = q.shape
    heads = range(width // HD_C)
    pair = lambda h: slice(LANES * (h // 2), LANES * (h // 2 + 1))
    first = lax.broadcasted_iota(jnp.int32, (tq, LANES), 1) < HD_C
    zero = jnp.zeros((tq, LANES), q.dtype)
    qs = [jnp.where(first, q[:, pair(h)], zero) if h % 2 == 0 else jnp.where(first, zero, q[:, pair(h)])
          for h in heads]
    bias = bias_ref[0]
    biases = [bias[:, HD_C * h:HD_C * h + 1] for h in heads]
    ui = ui_ref[...]
    row = lax.broadcasted_iota(jnp.int32, (tq, tq), 0)
    col = lax.broadcasted_iota(jnp.int32, (tq, tq), 1)
    causal = col < row

    def tile(j, masked):
        start = pl.multiple_of(j * tq, tq)
        kblk = k_ref[0, pl.ds(start, tq), :]
        vblk = v_ref[0, pl.ds(start, tq), :]
        zs = [lax.dot_general(qs[h], kblk[:, pair(h)], (((1,), (1,)), ((), ())), preferred_element_type=F32)
              + biases[h] for h in heads]
        incls = []
        for z in zs:
            sp = _softplus(z)
            if masked:
                sp = jnp.where(causal, sp, 0.0)
            incls.append(jnp.dot(sp.astype(BF16), ui, preferred_element_type=F32))
        for h in heads:
            if masked:
                e = jnp.where(causal, jnp.exp(zs[h] - incls[h]), 0.0)
            else:
                e = jnp.exp(zs[h] - incls[h] - run_ref[h])
            av = jnp.dot(e.astype(BF16), vblk[:, pair(h)], preferred_element_type=F32)
            if masked:
                acc_ref[h] = av
                run_ref[h] = incls[h][:, 0:1]
            else:
                acc_ref[h] += av
                run_ref[h] += incls[h][:, 0:1]

    tile(i, True)

    @pl.loop(0, i)
    def _(jj):
        tile(i - 1 - jj, False)

    o = jnp.concatenate([jnp.where(first, acc_ref[h], acc_ref[h + 1]) for h in heads[::2]], axis=-1)
    o_ref[0] = o * _silu(g_ref[0])


def _sb_prompt(q_bf, k_bf, v_bf, g_c, sb_bias):
    bsz, t, _ = q_bf.shape
    tq = min(SB_TILE, t)
    nq = t // tq
    width = SB_HEADS * HD_C
    groups = D_C // width
    bias_lanes = jnp.repeat(sb_bias.astype(F32), HD_C).reshape(groups, 1, width)
    ui = jnp.asarray(np.tril(np.ones((tq, tq), np.float32)), BF16)
    qspec = pl.BlockSpec((1, tq, width), lambda b, p, i: (b, i, p))
    kvspec = pl.BlockSpec((1, t, width), lambda b, p, i: (b, 0, p))
    return pl.pallas_call(
        _sb_prompt_body,
        grid=(bsz, groups, nq),
        in_specs=[qspec, kvspec, kvspec, qspec,
                  pl.BlockSpec((1, 1, width), lambda b, p, i: (p, 0, 0)),
                  pl.BlockSpec((tq, tq), lambda b, p, i: (0, 0))],
        out_specs=qspec,
        out_shape=jax.ShapeDtypeStruct((bsz, t, D_C), F32),
        scratch_shapes=[pltpu.VMEM((SB_HEADS, tq, LANES), F32), pltpu.VMEM((SB_HEADS, tq, 1), F32)],
        compiler_params=_cparams("arbitrary", "arbitrary", "arbitrary"),
        name="sb_prompt",
    )(q_bf, k_bf, v_bf, g_c, bias_lanes, ui)


def _sb_paged_body(pt_ref, qb_ref, g_ref, bias_ref, tri_ref, *refs):
    npg = PAGES_PER_STEP
    k_refs = refs[:npg]
    v_refs = refs[npg:2 * npg]
    o_ref = refs[2 * npg]
    acc, run = refs[2 * npg + 1:]
    j = pl.program_id(1)

    @pl.when(j == 0)
    def _():
        acc[...] = jnp.zeros_like(acc)
        run[...] = jnp.zeros_like(run)

    bias = bias_ref[:, 0:1]
    heads = range(H_C)
    for p in range(npg):
        z = jnp.concatenate([jnp.sum(k_refs[p][0, h] * qb_ref[0, h], axis=0, keepdims=True) for h in heads],
                            axis=0) + bias
        incl = _mm_sel(_softplus(z), tri_ref[...], 2)
        e = jnp.exp(z - incl - run[...])
        for h in heads:
            acc[h] += v_refs[p][0, h] * e[h:h + 1, :]
        run[...] += incl[:, 0:1]

    @pl.when(j == pl.num_programs(1) - 1)
    def _():
        ones = jnp.ones((8, PAGE), BF16)
        rows = []
        for h in heads:
            parts = [lax.dot_general(ones, part, (((1,), (1,)), ((), ())), preferred_element_type=F32)
                     for part in _split(acc[h], 3)]
            rows.append((parts[0] + parts[1] + parts[2])[0:1, :])
        o_ref[0] = jnp.concatenate(rows, axis=0) * _silu(g_ref[0])


def _sb_paged(q_s, g_s, cache_kt, cache_vt, page_table, sb_bias):
    bsz, n_pages = page_table.shape
    npg = PAGES_PER_STEP
    steps = n_pages // npg
    tri = jnp.asarray(np.tril(np.ones((PAGE, PAGE), np.float32)), BF16)
    bias = jnp.broadcast_to(sb_bias.astype(F32)[:, None], (H_C, LANES))
    qb = jnp.broadcast_to(q_s.astype(F32)[..., None], (bsz, H_C, HD_C, PAGE))

    def page_spec(p):
        return pl.BlockSpec((1, H_C, HD_C, PAGE),
                            lambda b, j, pt: (pt[b, n_pages - 1 - (j * npg + p)], 0, 0, 0))

    per_seq = pl.BlockSpec((1, H_C, HD_C), lambda b, j, pt: (b, 0, 0))
    full = lambda shape: pl.BlockSpec(shape, lambda b, j, pt: (0,) * len(shape))
    grid_spec = pltpu.PrefetchScalarGridSpec(
        num_scalar_prefetch=1,
        grid=(bsz, steps),
        in_specs=[pl.BlockSpec((1, H_C, HD_C, PAGE), lambda b, j, pt: (b, 0, 0, 0)), per_seq,
                  full((H_C, LANES)), full((PAGE, PAGE))] + [page_spec(p) for p in range(npg)] * 2,
        out_specs=per_seq,
        scratch_shapes=[pltpu.VMEM((H_C, HD_C, PAGE), F32), pltpu.VMEM((H_C, 1), F32)],
    )
    return pl.pallas_call(
        _sb_paged_body,
        grid_spec=grid_spec,
        out_shape=jax.ShapeDtypeStruct((bsz, H_C, HD_C), F32),
        compiler_params=_cparams("arbitrary", "arbitrary"),
        name="sb_paged",
    )(page_table, qb, g_s, bias, tri, *([cache_kt] * npg), *([cache_vt] * npg))


def _gla_body(q_ref, k_ref, v_ref, la_ref, gd_ref, s0_ref, dmat_ref, gng_ref, o_ref, sf_o, state):
    t_idx = pl.program_id(1)

    @pl.when(t_idx == 0)
    def _():
        state[...] = s0_ref[...]

    nseq, n, _ = q_ref.shape
    row = lax.broadcasted_iota(jnp.int32, (n, n), 0)
    col = lax.broadcasted_iota(jnp.int32, (n, n), 1)
    rsub = lax.broadcasted_iota(jnp.int32, (n, H_D * DK_D), 0)
    levels = _gla_levels(n)
    same_block = [(row >> int(math.log2(2 * m))) == (col >> int(math.log2(2 * m))) for m in levels]
    sls = [slice(DK_D * h, DK_D * (h + 1)) for h in range(H_D)]
    cast = lambda x: [x[:, sl].astype(BF16) for sl in sls]

    q_h, k_h, q_in, k_out, e_last, vh = [], [], [], [], [], []
    q_lv = [[] for _ in levels]
    k_lv = [[] for _ in levels]
    for s in range(nseq):
        q = q_ref[s]
        k = k_ref[s]
        dec = jnp.exp(_sel_mm(dmat_ref[...], la_ref[s], 3))
        blk = lambda idx: dec[n * idx:n * (idx + 1), :]
        q_h += cast(q)
        k_h += cast(k)
        q_in += cast(q * blk(0))
        k_out += cast(k * blk(1))
        e_last += [blk(0)[n - 1:n, sl] for sl in sls]
        for li, m in enumerate(levels):
            upper = (rsub & (2 * m - 1)) >= m
            e = blk(2 + li)
            q_lv[li] += cast(jnp.where(upper, q * e, 0.0))
            k_lv[li] += cast(jnp.where(upper, 0.0, k * e))
        v_all = v_ref[s]
        vh += [v_all[:, DV_D * h:DV_D * (h + 1)].astype(BF16) for h in range(H_D)]
    chains = range(nseq * H_D)
    st = [state[c // H_D, c % H_D] for c in chains]
    inter = [_mm_nt(q_in[c], st[c]) for c in chains]
    scores = [jnp.where(row == col, _mm_nt(q_h[c], k_h[c]), 0.0) for c in chains]
    for li in range(len(levels)):
        part = [_mm_nt(q_lv[li][c], k_lv[li][c]) for c in chains]
        scores = [scores[c] + jnp.where(same_block[li], part[c], 0.0) for c in chains]
    o = [inter[c] + _mm(scores[c], vh[c]) for c in chains]
    for c in chains:
        state[c // H_D, c % H_D] = st[c] * e_last[c] + _mm_tn(vh[c], k_out[c])
    outs = [oc * lax.rsqrt(jnp.mean(oc * oc, axis=-1, keepdims=True) + RMS_EPS) for oc in o]
    for s in range(nseq):
        o_ref[s] = jnp.concatenate(outs[s * H_D:(s + 1) * H_D], axis=-1) * gng_ref[...] * _silu(gd_ref[s])

    @pl.when(t_idx == pl.num_programs(1) - 1)
    def _():
        sf_o[...] = state[...]


def _gla_scan(q, k, v, la, gd, s0_t, gn_g, chunk):
    bsz, t, hk = q.shape
    nt = t // chunk
    nseq = GLA_SEQS_PER_STEP
    tok = lambda n: pl.BlockSpec((nseq, chunk, n), lambda b, i: (b, i, 0))
    st = pl.BlockSpec((nseq, H_D, DV_D, DK_D), lambda b, i: (b, 0, 0, 0))
    full = lambda shape: pl.BlockSpec(shape, lambda b, i: (0,) * len(shape))
    dmat = jnp.asarray(_gla_decay_rows(chunk), BF16)
    return pl.pallas_call(
        _gla_body,
        grid=(bsz // nseq, nt),
        in_specs=[tok(hk), tok(hk), tok(D_D), tok(hk), tok(D_D), st, full(dmat.shape), full((1, D_D))],
        out_specs=[tok(D_D), st],
        out_shape=[jax.ShapeDtypeStruct((bsz, t, D_D), F32),
                   jax.ShapeDtypeStruct((bsz, H_D, DV_D, DK_D), F32)],
        scratch_shapes=[pltpu.VMEM((nseq, H_D, DV_D, DK_D), F32)],
        compiler_params=_cparams("arbitrary", "arbitrary"),
        name="gla_scan_c%d" % chunk,
    )(q, k, v, la, gd, s0_t, dmat, gn_g.reshape(1, -1))


def _pad_tokens(a, n):
    return jnp.pad(a, ((0, 0), (0, n - a.shape[1]), (0, 0)))


def kernel(x_prompt, x_sample, state_rwkv, state_shift, state_s5_re, state_s5_im, cache_k, cache_v, state_gla,
           page_table, ln_g, ln_b, w_in_e, mu_shift, rwkv_w0, rwkv_w2, rwkv_a0, rwkv_a2, rwkv_k_k, rwkv_k_a,
           rwkv_r_k, rwkv_gn_g, rwkv_gn_b, s5_lambda_re, s5_lambda_im, s5_log_dt, s5_b_re, s5_b_im, s5_c_re,
           s5_c_im, s5_d, s5_glu_w, s5_glu_b, w_out_e, w_in_o, sb_bias, gla_alpha_w, gla_alpha_b, gla_gn_g,
           w_out_o):
    bp, t_p, _ = x_prompt.shape
    bs = x_sample.shape[0]
    n_phys = cache_k.shape[1]
    xs_rows = x_sample.reshape(1, bs, D_MODEL)

    w_in_bf = w_in_e[0].astype(BF16)
    prep_w = (w_in_bf, mu_shift[0], rwkv_w0[0], rwkv_w2[0], rwkv_a0[0], rwkv_a2[0], rwkv_k_k[0], rwkv_k_a[0])
    pe = _even_prep(x_prompt, jnp.zeros((bp, 1, N_SHIFT), F32), *prep_w, decode=False)
    se = _even_prep(xs_rows, state_shift[0].reshape(1, bs, N_SHIFT), *prep_w, decode=True)
    rw_w = (rwkv_gn_g[0], rwkv_gn_b[0], rwkv_r_k[0])
    oa_p, rw_p = _rwkv_scan(*pe[:7], jnp.zeros((bp, H_A, HD_A, HD_A), F32), *rw_w, chunk=RWKV_CHUNK)
    se_tok = [_pad_tokens(a.reshape(bs, 1, D_A), DECODE_CHUNK) for a in se[:7]]
    oa_s, rw_s = _rwkv_scan(*se_tok, state_rwkv[0], *rw_w, chunk=DECODE_CHUNK)

    abar_re, abar_im, bbr, bbi, cr, ci = _s5_params(s5_lambda_re[0], s5_lambda_im[0], s5_log_dt[0], s5_b_re[0],
                                                    s5_b_im[0], s5_c_re[0], s5_c_im[0])
    s5_tail = (cr.astype(BF16), ci.astype(BF16), s5_d[0], s5_glu_w[0].astype(BF16), s5_glu_b[0])
    zst = jnp.zeros((bp, 1, N_STATE_B), F32)
    ob_p, re_p, im_p = _s5_mixer(pe[7], pe[8], zst, zst, abar_re, abar_im, bbr.astype(BF16), bbi.astype(BF16),
                                 *s5_tail, decode=False)
    ob_s, re_s, im_s = _s5_mixer(se[7], se[8], state_s5_re[0].reshape(1, bs, N_STATE_B),
                                 state_s5_im[0].reshape(1, bs, N_STATE_B), abar_re, abar_im, bbr, bbi,
                                 *s5_tail, decode=True)

    w_out_e_bf = w_out_e[0].astype(BF16)
    x1_p = _out_norm(oa_p, ob_p, x_prompt, w_out_e_bf, ln_g[0], ln_b[0])
    x1_s = _out_norm(oa_s[:, 0:1, :].reshape(1, bs, D_A), ob_s, xs_rows, w_out_e_bf, ln_g[0], ln_b[0])

    w = w_in_o[0]
    w_o_bf = jnp.concatenate([w[:, :O_GD], w[:, O_GD + R_G:], w[:, O_GD:O_GD + R_G],
                              jnp.zeros((D_MODEL, N_PROJ_O_STAGED - w.shape[1]), F32)], axis=1).astype(BF16)
    alpha_w_pad = jnp.pad(gla_alpha_w[0], ((0, LANES - R_G), (0, 0)))
    po = _odd_prep(x1_p, w_o_bf, alpha_w_pad, gla_alpha_b[0])
    so = _odd_prep(x1_s, w_o_bf, alpha_w_pad, gla_alpha_b[0])
    qc_p, kc_p, vc_p, kcb_p, vcb_p, gc_p, qd_p, kd_p, vd_p, la_p, gd_p = po
    qc_s, kc_s, vc_s, _, _, gc_s, qd_s, kd_s, vd_s, la_s, gd_s = so

    oc_p = _sb_prompt(qc_p, kcb_p, vcb_p, gc_p, sb_bias[0])
    oc_s = _sb_paged(qc_s.reshape(bs, H_C, HD_C), gc_s.reshape(bs, H_C, HD_C),
                     jnp.transpose(cache_k[0], (0, 2, 3, 1)), jnp.transpose(cache_v[0], (0, 2, 3, 1)),
                     page_table, sb_bias[0])

    od_p, gl_p = _gla_scan(qd_p, kd_p, vd_p, la_p, gd_p, jnp.zeros((bp, H_D, DV_D, DK_D), F32), gla_gn_g[0],
                           chunk=GLA_CHUNK)
    tok_s = [_pad_tokens(a.reshape(bs, 1, a.shape[-1]), DECODE_CHUNK) for a in (qd_s, kd_s, vd_s, la_s, gd_s)]
    od_s, gl_s = _gla_scan(*tok_s, jnp.swapaxes(state_gla[0], -1, -2), gla_gn_g[0], chunk=DECODE_CHUNK)

    w_out_o_bf = w_out_o[0].astype(BF16)
    y_p = _out_norm(oc_p, od_p, x1_p, w_out_o_bf, ln_g[1], ln_b[1])
    y_s = _out_norm(oc_s.reshape(1, bs, D_C), od_s[:, 0:1, :].reshape(1, bs, D_D), x1_s, w_out_o_bf, ln_g[1],
                    ln_b[1])

    heads_c = lambda a, b, t: a.reshape(1, b, t, H_C, HD_C)
    return (y_p, y_s.reshape(bs, 1, D_MODEL),
            rw_p[None], rw_s[None],
            pe[9].reshape(1, bp, N_SHIFT), se[9].reshape(1, bs, N_SHIFT),
            re_p.reshape(1, bp, G_B, P_B), re_s.reshape(1, bs, G_B, P_B),
            im_p.reshape(1, bp, G_B, P_B), im_s.reshape(1, bs, G_B, P_B),
            heads_c(kc_p, bp, t_p), heads_c(kc_s, bs, 1), heads_c(vc_p, bp, t_p), heads_c(vc_s, bs, 1),
            jnp.swapaxes(gl_p, -1, -2)[None], jnp.swapaxes(gl_s, -1, -2)[None])
```

```python
import functools
import math

import numpy as np
import jax
import jax.numpy as jnp
from jax import lax
from jax.experimental import pallas as pl
from jax.experimental.pallas import tpu as pltpu

F32 = jnp.float32
BF16 = jnp.bfloat16

D_MODEL = 1024
D_A = 512
HD_A = 64
H_A = 8
R_W = 64
R_A = 64
D_B = 512
S5_GROUP = 16
G_B = 32
P_B = 64
N_STATE_B = G_B * P_B
D_C = 512
HD_C = 64
H_C = 8
D_D = 512
H_D = 4
DV_D = 128
DK_D = 64
R_G = 16
GLA_TAU = 16.0
N_SHIFT = 3 * D_A + R_W + R_A
N_PROJ_E = N_SHIFT + D_A + 2 * D_B
DEPTH = 2
DEEPNORM_ALPHA = (2.0 * DEPTH) ** 0.25
LN_EPS = 1e-5
GN_EPS = 64e-5
RMS_EPS = 1e-5
PAGE = 128

LOG2E = 1.4426950408889634
LANES = 128
VMEM_LIMIT = 56 * 1024 * 1024

PROJ_BLOCK = 512
OUT_BLOCK = 1024
S5_BLOCK = 256
RWKV_CHUNK = 64
RWKV_SEQS_PER_STEP = 4
GLA_CHUNK = 64
GLA_SEQS_PER_STEP = 4
DECODE_CHUNK = 8
SB_TILE = 256
SB_HEADS = 8
PAGES_PER_STEP = 16


def _cparams(*sem):
    return pltpu.CompilerParams(dimension_semantics=sem, vmem_limit_bytes=VMEM_LIMIT)


def _mm(a, b):
    return jnp.dot(a.astype(BF16), b.astype(BF16), preferred_element_type=F32)


def _mm_nt(a, b):
    return lax.dot_general(a.astype(BF16), b.astype(BF16), (((1,), (1,)), ((), ())),
                           preferred_element_type=F32)


def _mm_tn(a, b):
    return lax.dot_general(a.astype(BF16), b.astype(BF16), (((0,), (0,)), ((), ())),
                           preferred_element_type=F32)


def _mm_f32(a, b):
    return jnp.dot(a, b, precision=lax.Precision.HIGHEST, preferred_element_type=F32)


def _mmw(a, w):
    if w.dtype == BF16:
        return jnp.dot(a.astype(BF16), w, preferred_element_type=F32)
    return _mm_f32(a, w)


def _split(x, parts):
    out = []
    rem = x
    for _ in range(parts - 1):
        p = rem.astype(BF16)
        out.append(p)
        rem = rem - p.astype(F32)
    out.append(rem.astype(BF16))
    return out


def _sel_mm(sel, x, parts):
    acc = None
    for p in _split(x, parts):
        t = jnp.dot(sel, p, preferred_element_type=F32)
        acc = t if acc is None else acc + t
    return acc


def _mm_sel(x, sel, parts):
    acc = None
    for p in _split(x, parts):
        t = jnp.dot(p, sel, preferred_element_type=F32)
        acc = t if acc is None else acc + t
    return acc


def _sigmoid(t):
    return 1.0 / (1.0 + jnp.exp(-t))


def _silu(t):
    return t * _sigmoid(t)


def _log1pexp_negabs(t):
    return jnp.log(1.0 + jnp.exp2(jnp.abs(t) * (-LOG2E)))


def _softplus(t):
    return jnp.maximum(t, 0.0) + _log1pexp_negabs(t)


def _gelu_tanh(x):
    return 0.5 * x * (1.0 + jnp.tanh(math.sqrt(2.0 / math.pi) * (x + 0.044715 * (x * x * x))))


def _tri_incl(n):
    return np.tril(np.ones((n, n), np.float32))


def _block_ones(n, blk):
    idx = np.arange(n) // blk
    return (idx[:, None] == idx[None, :]).astype(np.float32)


def _gla_decay_rows(n):
    t = np.arange(n)[:, None]
    i = np.arange(n)[None, :]
    mats = [(i <= t), (i > t)]
    m = n // 2
    while m >= 1:
        p = 2 * m * (t // (2 * m)) + m - 1
        upper = (t % (2 * m)) >= m
        mats.append(np.where(upper, (i > p) & (i <= t), (i > t) & (i <= p)))
        m //= 2
    return np.concatenate([x.astype(np.float32) for x in mats], axis=0)


def _gla_levels(n):
    out = []
    m = n // 2
    while m >= 1:
        out.append(m)
        m //= 2
    return out


def _even_prep_body(decode, x_ref, zp_ref, w_ref, mu_ref, w0_ref, w2_ref, a0_ref, a2_ref, kk_ref, ka_ref,
                    ones_ref, r_o, lw_o, k_o, v_o, aa_o, bb_o, ga_o, u_o, gb_o, sh_o, carry):
    x = x_ref[0].astype(BF16)
    proj = jnp.dot(x, w_ref[...], preferred_element_type=F32)
    z = proj[:, :N_SHIFT]
    tb = z.shape[0]
    if decode:
        zp = zp_ref[0]
        sh_o[0] = z
    else:
        @pl.when(pl.program_id(1) == 0)
        def _():
            carry[...] = zp_ref[0]
        row = lax.broadcasted_iota(jnp.int32, z.shape, 0)
        zp = jnp.where(row == 0, carry[...], pltpu.roll(z, 1, 0))
        carry[...] = z[tb - 1:tb, :]
        sh_o[0] = z[tb - 1:tb, :]
    zm = z + mu_ref[...] * (zp - z)
    r = zm[:, 0:D_A]
    k = zm[:, D_A:2 * D_A]
    v = zm[:, 2 * D_A:3 * D_A]
    wd = zm[:, 3 * D_A:3 * D_A + R_W]
    ad = zm[:, 3 * D_A + R_W:N_SHIFT]
    w = -_softplus(-(w0_ref[...] + _mm(jnp.tanh(wd), w2_ref[...]))) - 0.5
    a = _sigmoid(a0_ref[...] + _mm(ad, a2_ref[...]))
    kk = k * kk_ref[...]
    ss = _mm_sel(kk * kk, ones_ref[...], 2)
    kkn = kk / jnp.maximum(jnp.sqrt(ss), 1e-12)
    r_o[0] = r
    lw_o[0] = -jnp.exp(w)
    k_o[0] = k * (1.0 + (a - 1.0) * ka_ref[...])
    v_o[0] = v
    aa_o[0] = -kkn
    bb_o[0] = kkn * a
    ga_o[0] = proj[:, N_SHIFT:N_SHIFT + D_A]
    u_o[0] = proj[:, N_SHIFT + D_A:N_SHIFT + D_A + D_B]
    gb_o[0] = proj[:, N_SHIFT + D_A + D_B:N_PROJ_E]


def _even_prep(x, zprev, w_bf, mu, w0, w2, a0, a2, k_k, k_a, decode):
    bsz, t, _ = x.shape
    tb = t if decode else min(PROJ_BLOCK, t)
    nt = t // tb
    row2 = lambda a: a.reshape(1, -1)
    ones = jnp.asarray(_block_ones(D_A, HD_A), BF16)
    full = lambda shape: pl.BlockSpec(shape, lambda b, i: (0,) * len(shape))
    tok = lambda n: pl.BlockSpec((1, tb, n), lambda b, i: (b, i, 0))
    zp_spec = (pl.BlockSpec((1, tb, N_SHIFT), lambda b, i: (b, i, 0)) if decode
               else pl.BlockSpec((1, 1, N_SHIFT), lambda b, i: (b, 0, 0)))
    sh_rows = tb if decode else 1
    outs = [jax.ShapeDtypeStruct((bsz, t, D_A), F32)] * 9 + [jax.ShapeDtypeStruct((bsz, sh_rows, N_SHIFT), F32)]
    out_specs = [tok(D_A)] * 9 + [pl.BlockSpec((1, sh_rows, N_SHIFT), lambda b, i: (b, 0, 0))]
    return pl.pallas_call(
        functools.partial(_even_prep_body, decode),
        grid=(bsz, nt),
        in_specs=[tok(D_MODEL), zp_spec, full((D_MODEL, N_PROJ_E)), full((1, N_SHIFT)), full((1, D_A)),
                  full((R_W, D_A)), full((1, D_A)), full((R_A, D_A)), full((1, D_A)), full((1, D_A)),
                  full((D_A, D_A))],
        out_specs=out_specs,
        out_shape=outs,
        scratch_shapes=[pltpu.VMEM((1, N_SHIFT), F32)],
        compiler_params=_cparams("arbitrary", "arbitrary"),
        name="even_prep_decode" if decode else "even_prep",
    )(x, zprev, w_bf, row2(mu), row2(w0), w2, row2(a0), a2, row2(k_k), row2(k_a), ones)


def _rwkv_body(r_ref, lw_ref, k_ref, v_ref, aa_ref, bb_ref, ga_ref, s0_ref, tri_ref, ones_ref, gng_ref, gnb_ref,
               rk_ref, oa_o, sf_o, state):
    t_idx = pl.program_id(1)

    @pl.when(t_idx == 0)
    def _():
        state[...] = s0_ref[...]

    nseq, n, _ = lw_ref.shape
    row = lax.broadcasted_iota(jnp.int32, (n, n), 0)
    col = lax.broadcasted_iota(jnp.int32, (n, n), 1)
    strict = col < row
    incl = col <= row
    levels = int(math.log2(n))
    sls = [slice(HD_A * h, HD_A * (h + 1)) for h in range(H_A)]
    cast = lambda x: [x[:, sl].astype(BF16) for sl in sls]

    a_h, b_h, k_h, r_h, v_h, b_lh, k_lh, p_last_h, v_seq, rkr_seq = [], [], [], [], [], [], [], [], [], []
    for s in range(nseq):
        lw = lw_ref[s]
        c = _sel_mm(tri_ref[...], lw, 3)
        c_last = c[n - 1:n, :]
        r = r_ref[s]
        k = k_ref[s]
        v = v_ref[s]
        bb = bb_ref[s]
        pinv = jnp.exp(-c)
        tail = jnp.exp(c_last - c)
        p_last = jnp.exp(c_last)
        a_h += cast(aa_ref[s] * jnp.exp(c - lw))
        b_h += cast(bb * pinv)
        k_h += cast(k * pinv)
        r_h += cast(r * jnp.exp(c))
        v_h += cast(v)
        b_lh += cast(bb * tail)
        k_lh += cast(k * tail)
        p_last_h += [p_last[:, sl] for sl in sls]
        v_seq.append(v)
        rkr_seq.append(r * k * rk_ref[...])
    chains = range(nseq * H_A)
    s_h = [state[c // H_A, c % H_A] for c in chains]
    s_bf = [x.astype(BF16) for x in s_h]
    nmat = [jnp.where(strict, _mm_nt(a_h[c], b_h[c]), 0.0) for c in chains]
    mmat = [jnp.where(strict, _mm_nt(a_h[c], k_h[c]), 0.0) for c in chains]
    rbm = [jnp.where(incl, _mm_nt(r_h[c], b_h[c]), 0.0) for c in chains]
    rkm = [jnp.where(incl, _mm_nt(r_h[c], k_h[c]), 0.0) for c in chains]
    u = [_mm_nt(a_h[c], s_bf[c]) + _mm(mmat[c], v_h[c]) for c in chains]
    y0 = [_mm_nt(r_h[c], s_bf[c]) + _mm(rkm[c], v_h[c]) for c in chains]
    npow = nmat
    for lvl in range(levels):
        u = [u[c] + _mm(npow[c], u[c]) for c in chains]
        if lvl + 1 < levels:
            npow = [_mm(npow[c], npow[c]) for c in chains]
    y = [y0[c] + _mm(rbm[c], u[c]) for c in chains]
    for c in chains:
        state[c // H_A, c % H_A] = s_h[c] * p_last_h[c] + _mm_tn(u[c], b_lh[c]) + _mm_tn(v_h[c], k_lh[c])
    for s in range(nseq):
        yn = []
        for yh in y[s * H_A:(s + 1) * H_A]:
            mu = jnp.mean(yh, axis=-1, keepdims=True)
            var = jnp.mean(jnp.square(yh - mu), axis=-1, keepdims=True)
            yn.append((yh - mu) * lax.rsqrt(var + GN_EPS))
        yn = jnp.concatenate(yn, axis=-1)
        rk_head = _mm_sel(rkr_seq[s], ones_ref[...], 2)
        oa_o[s] = (yn * gng_ref[...] + gnb_ref[...] + rk_head * v_seq[s]) * _silu(ga_ref[s])

    @pl.when(t_idx == pl.num_programs(1) - 1)
    def _():
        sf_o[...] = state[...]


def _rwkv_scan(r, lw, k, v, aa, bb, ga, s0, gn_g, gn_b, r_k, chunk):
    bsz, t, _ = r.shape
    nt = t // chunk
    nseq = RWKV_SEQS_PER_STEP
    tok = pl.BlockSpec((nseq, chunk, D_A), lambda b, i: (b, i, 0))
    st = pl.BlockSpec((nseq, H_A, HD_A, HD_A), lambda b, i: (b, 0, 0, 0))
    full = lambda shape: pl.BlockSpec(shape, lambda b, i: (0,) * len(shape))
    tri = jnp.asarray(_tri_incl(chunk), BF16)
    ones = jnp.asarray(_block_ones(D_A, HD_A), BF16)
    return pl.pallas_call(
        _rwkv_body,
        grid=(bsz // nseq, nt),
        in_specs=[tok] * 7 + [st, full((chunk, chunk)), full((D_A, D_A)), full((1, D_A)), full((1, D_A)),
                  full((1, D_A))],
        out_specs=[tok, st],
        out_shape=[jax.ShapeDtypeStruct((bsz, t, D_A), F32),
                   jax.ShapeDtypeStruct((bsz, H_A, HD_A, HD_A), F32)],
        scratch_shapes=[pltpu.VMEM((nseq, H_A, HD_A, HD_A), F32)],
        compiler_params=_cparams("arbitrary", "arbitrary"),
        name="rwkv_scan_c%d" % chunk,
    )(r, lw, k, v, aa, bb, ga, s0, tri, ones, gn_g.reshape(1, -1), gn_b.reshape(1, -1), r_k.reshape(1, -1))


S5_LANE_BLOCKS = D_B // LANES
S5_STATES_PER_BLOCK = N_STATE_B // S5_LANE_BLOCKS
S5_ROWS = 8


def _s5_body(decode, u_ref, gb_ref, h0r_ref, h0i_ref, ar_ref, ai_ref, bbr_ref, bbi_ref, cr_ref, ci_ref, d_ref,
             gw_ref, gbias_ref, ob_o, hr_o, hi_o, car_r, car_i):
    u = u_ref[0]
    tb = u.shape[0]
    if not decode:
        @pl.when(pl.program_id(1) == 0)
        def _():
            car_r[...] = h0r_ref[0]
            car_i[...] = h0i_ref[0]
        sub = lax.broadcasted_iota(jnp.int32, (1, S5_ROWS, S5_STATES_PER_BLOCK), 1)
    ys = []
    for j in range(S5_LANE_BLOCKS):
        sl = slice(S5_STATES_PER_BLOCK * j, S5_STATES_PER_BLOCK * (j + 1))
        uj = u[:, LANES * j:LANES * (j + 1)]
        bur = _mmw(uj, bbr_ref[j])
        bui = _mmw(uj, bbi_ref[j])
        pw_r = ar_ref[:, sl]
        pw_i = ai_ref[:, sl]
        ar, ai = pw_r[0:1, :], pw_i[0:1, :]
        if decode:
            h0r = h0r_ref[0][:, sl]
            h0i = h0i_ref[0][:, sl]
            hr = ar * h0r - ai * h0i + bur
            hi = ar * h0i + ai * h0r + bui
            hr_o[0, :, sl] = hr
            hi_o[0, :, sl] = hi
        else:
            width = S5_STATES_PER_BLOCK
            groups = tb // S5_ROWS
            hr3 = bur.reshape(groups, S5_ROWS, width)
            hi3 = bui.reshape(groups, S5_ROWS, width)
            pr, pi = ar.reshape(1, 1, width), ai.reshape(1, 1, width)
            s = 1
            while s < S5_ROWS:
                keep = sub >= s
                mr = jnp.where(keep, pr, 0.0)
                mi = jnp.where(keep, pi, 0.0)
                sr = pltpu.roll(hr3, s, 1)
                si = pltpu.roll(hi3, s, 1)
                hr3, hi3 = hr3 + mr * sr - mi * si, hi3 + mr * si + mi * sr
                pr, pi = pr * pr - pi * pi, 2.0 * pr * pi
                s *= 2
            cr = car_r[:, sl]
            ci = car_i[:, sl]
            rows_r, rows_i = [], []
            for g in range(groups):
                gr = hr3[g] + pw_r * cr - pw_i * ci
                gi = hi3[g] + pw_r * ci + pw_i * cr
                cr, ci = gr[S5_ROWS - 1:S5_ROWS, :], gi[S5_ROWS - 1:S5_ROWS, :]
                rows_r.append(gr)
                rows_i.append(gi)
            hr = jnp.concatenate(rows_r, axis=0)
            hi = jnp.concatenate(rows_i, axis=0)
            car_r[:, sl] = cr
            car_i[:, sl] = ci
            hr_o[0, :, sl] = cr
            hi_o[0, :, sl] = ci
        ys.append(_mm(hr, cr_ref[j]) - _mm(hi, ci_ref[j]))
    y = jnp.concatenate(ys, axis=-1) + d_ref[...] * u
    y = _gelu_tanh(y)
    y = y * _sigmoid(_mm(y, gw_ref[...]) + gbias_ref[...])
    ob_o[0] = y * _silu(gb_ref[0])


def _s5_mixer(u, gate_b, h0r, h0i, abar_re, abar_im, bbr, bbi, cr, ci, d_skip, glu_w, glu_b, decode):
    bsz, t, _ = u.shape
    tb = t if decode else min(S5_BLOCK, t)
    nt = t // tb
    st_rows = tb if decode else 1
    tok = pl.BlockSpec((1, tb, D_B), lambda b, i: (b, i, 0))
    st = pl.BlockSpec((1, st_rows, N_STATE_B), lambda b, i: (b, 0, 0))
    full = lambda shape: pl.BlockSpec(shape, lambda b, i: (0,) * len(shape))
    return pl.pallas_call(
        functools.partial(_s5_body, decode),
        grid=(bsz, nt),
        in_specs=[tok, tok, st, st, full((S5_ROWS, N_STATE_B)), full((S5_ROWS, N_STATE_B)),
                  full(bbr.shape), full(bbi.shape), full(cr.shape), full(ci.shape),
                  full((1, D_B)), full((D_B, D_B)), full((1, D_B))],
        out_specs=[tok, st, st],
        out_shape=[jax.ShapeDtypeStruct((bsz, t, D_B), F32),
                   jax.ShapeDtypeStruct((bsz, st_rows, N_STATE_B), F32),
                   jax.ShapeDtypeStruct((bsz, st_rows, N_STATE_B), F32)],
        scratch_shapes=[pltpu.VMEM((1, N_STATE_B), F32), pltpu.VMEM((1, N_STATE_B), F32)],
        compiler_params=_cparams("arbitrary", "arbitrary"),
        name="s5_decode" if decode else "s5_scan",
    )(u, gate_b, h0r, h0i, abar_re, abar_im, bbr, bbi, cr, ci, d_skip.reshape(1, -1), glu_w,
      glu_b.reshape(1, -1))


def _s5_params(lam_re, lam_im, log_dt, b_re, b_im, c_re, c_im):
    dt = jnp.exp(log_dt)[:, None]
    mag = jnp.exp(lam_re * dt)
    abar_re, abar_im = mag * jnp.cos(lam_im * dt), mag * jnp.sin(lam_im * dt)
    den = lam_re * lam_re + lam_im * lam_im
    nr, ni = abar_re - 1.0, abar_im
    f_re, f_im = (nr * lam_re + ni * lam_im) / den, (ni * lam_re - nr * lam_im) / den
    bb_re = f_re[..., None] * b_re - f_im[..., None] * b_im
    bb_im = f_re[..., None] * b_im + f_im[..., None] * b_re
    gpb = G_B // S5_LANE_BLOCKS
    eye = jnp.eye(gpb, dtype=F32)

    def stage_b(m):
        m = m.reshape(S5_LANE_BLOCKS, gpb, P_B, S5_GROUP)
        blk = jnp.einsum('jgpc,gh->jgchp', m, eye)
        return blk.reshape(S5_LANE_BLOCKS, gpb * S5_GROUP, gpb * P_B)

    def stage_c(m):
        m = m.reshape(S5_LANE_BLOCKS, gpb, S5_GROUP, P_B)
        blk = jnp.einsum('jgcp,gh->jgphc', m, eye)
        return blk.reshape(S5_LANE_BLOCKS, gpb * P_B, gpb * S5_GROUP)

    pw_re, pw_im = [abar_re.reshape(1, -1)], [abar_im.reshape(1, -1)]
    for _ in range(S5_ROWS - 1):
        pr, pi = pw_re[-1], pw_im[-1]
        pw_re.append(pr * pw_re[0] - pi * pw_im[0])
        pw_im.append(pr * pw_im[0] + pi * pw_re[0])
    return (jnp.concatenate(pw_re, axis=0), jnp.concatenate(pw_im, axis=0), stage_b(bb_re), stage_b(bb_im),
            stage_c(c_re), stage_c(c_im))


def _out_norm_body(a_ref, b_ref, x_ref, w_ref, g_ref, beta_ref, o_ref):
    half = a_ref.shape[-1]
    out = _mmw(a_ref[0], w_ref[0:half, :]) + _mmw(b_ref[0], w_ref[half:2 * half, :])
    h = DEEPNORM_ALPHA * x_ref[0] + out
    mu = jnp.mean(h, axis=-1, keepdims=True)
    var = jnp.mean(jnp.square(h - mu), axis=-1, keepdims=True)
    o_ref[0] = (h - mu) * lax.rsqrt(var + LN_EPS) * g_ref[...] + beta_ref[...]


def _out_norm(a, b, x, w_bf, g, beta):
    bsz, t, half = a.shape
    tb = min(OUT_BLOCK, t)
    nt = t // tb
    tok = lambda n: pl.BlockSpec((1, tb, n), lambda bi, i: (bi, i, 0))
    full = lambda shape: pl.BlockSpec(shape, lambda bi, i: (0,) * len(shape))
    return pl.pallas_call(
        _out_norm_body,
        grid=(bsz, nt),
        in_specs=[tok(half), tok(half), tok(D_MODEL), full((2 * half, D_MODEL)), full((1, D_MODEL)),
                  full((1, D_MODEL))],
        out_specs=tok(D_MODEL),
        out_shape=jax.ShapeDtypeStruct((bsz, t, D_MODEL), F32),
        compiler_params=_cparams("arbitrary", "arbitrary"),
        name="out_norm",
    )(a, b, x, w_bf, g.reshape(1, -1), beta.reshape(1, -1))


O_QC, O_KC, O_VC, O_GC = 0, 512, 1024, 1536
O_QD, O_KD, O_VD, O_GD, O_AD = 2048, 2304, 2560, 3072, 3584
N_PROJ_O_STAGED = 3712


def _odd_prep_body(a_ref, b_ref, x_ref, wo_ref, g_ref, beta_ref, w_ref, aw_ref, ab_ref, x1_o, qc_o, kc_o, vc_o,
                   kcb_o, vcb_o, gc_o, qd_o, kd_o, vd_o, la_o, gd_o):
    _out_norm_body(a_ref, b_ref, x_ref, wo_ref, g_ref, beta_ref, x1_o)
    x = x1_o[0].astype(BF16)
    proj = jnp.dot(x, w_ref[...], preferred_element_type=F32)
    kc = proj[:, O_KC:O_VC]
    vc = proj[:, O_VC:O_GC]
    qc_o[0] = (proj[:, O_QC:O_KC] * (HD_C ** -0.5)).astype(BF16)
    kc_o[0] = kc
    vc_o[0] = vc
    kcb_o[0] = kc.astype(BF16)
    vcb_o[0] = vc.astype(BF16)
    gc_o[0] = proj[:, O_GC:O_QD]
    qd_o[0] = proj[:, O_QD:O_KD] * (DK_D ** -0.5)
    kd_o[0] = proj[:, O_KD:O_VD]
    vd_o[0] = proj[:, O_VD:O_GD]
    gd_o[0] = proj[:, O_GD:O_AD]
    pre = _mm(proj[:, O_AD:N_PROJ_O_STAGED], aw_ref[...]) + ab_ref[...]
    la_o[0] = -_softplus(-pre) * (1.0 / GLA_TAU)


def _norm_odd_prep(a, b, x, w_out_bf, g, beta, w_bf, alpha_w_pad, alpha_b):
    bsz, t, half = a.shape
    tb = min(PROJ_BLOCK, t)
    nt = t // tb
    tok = lambda n: pl.BlockSpec((1, tb, n), lambda bi, i: (bi, i, 0))
    full = lambda shape: pl.BlockSpec(shape, lambda bi, i: (0,) * len(shape))
    hk = H_D * DK_D
    widths = [(D_MODEL, F32), (D_C, BF16), (D_C, F32), (D_C, F32), (D_C, BF16), (D_C, BF16), (D_C, F32),
              (hk, F32), (hk, F32), (D_D, F32), (hk, F32), (D_D, F32)]
    return pl.pallas_call(
        _odd_prep_body,
        grid=(bsz, nt),
        in_specs=[tok(half), tok(half), tok(D_MODEL), full((2 * half, D_MODEL)), full((1, D_MODEL)),
                  full((1, D_MODEL)), full((D_MODEL, N_PROJ_O_STAGED)), full((LANES, hk)), full((1, hk))],
        out_specs=[tok(n) for n, _ in widths],
        out_shape=[jax.ShapeDtypeStruct((bsz, t, n), dt) for n, dt in widths],
        compiler_params=_cparams("arbitrary", "arbitrary"),
        name="norm_odd_prep",
    )(a, b, x, w_out_bf, g.reshape(1, -1), beta.reshape(1, -1), w_bf, alpha_w_pad, alpha_b.reshape(1, -1))


def _sb_prompt_body(q_ref, k_ref, v_ref, g_ref, bias_ref, ui_ref, o_ref, acc_ref, run_ref):
    i = pl.program_id(2)
    q = q_ref[0]
    tq, width = q.shape
    heads = range(width // HD_C)
    pair = lambda h: slice(LANES * (h // 2), LANES * (h // 2 + 1))
    first = lax.broadcasted_iota(jnp.int32, (tq, LANES), 1) < HD_C
    zero = jnp.zeros((tq, LANES), q.dtype)
    qs = [jnp.where(first, q[:, pair(h)], zero) if h % 2 == 0 else jnp.where(first, zero, q[:, pair(h)])
          for h in heads]
    bias = bias_ref[0]
    biases = [bias[:, HD_C * h:HD_C * h + 1] for h in heads]
    ui = ui_ref[...]
    row = lax.broadcasted_iota(jnp.int32, (tq, tq), 0)
    col = lax.broadcasted_iota(jnp.int32, (tq, tq), 1)
    causal = col < row

    def tile(j, masked):
        start = pl.multiple_of(j * tq, tq)
        kblk = k_ref[0, pl.ds(start, tq), :]
        vblk = v_ref[0, pl.ds(start, tq), :]
        zs = [lax.dot_general(qs[h], kblk[:, pair(h)], (((1,), (1,)), ((), ())), preferred_element_type=F32)
              + biases[h] for h in heads]
        incls = []
        for z in zs:
            sp = _softplus(z)
            if masked:
                sp = jnp.where(causal, sp, 0.0)
            incls.append(jnp.dot(sp.astype(BF16), ui, preferred_element_type=F32))
        for h in heads:
            if masked:
                e = jnp.where(causal, jnp.exp(zs[h] - incls[h]), 0.0)
            else:
                e = jnp.exp(zs[h] - incls[h] - run_ref[h])
            av = jnp.dot(e.astype(BF16), vblk[:, pair(h)], preferred_element_type=F32)
            if masked:
                acc_ref[h] = av
                run_ref[h] = incls[h][:, 0:1]
            else:
                acc_ref[h] += av
                run_ref[h] += incls[h][:, 0:1]

    tile(i, True)

    @pl.loop(0, i)
    def _(jj):
        tile(i - 1 - jj, False)

    o = jnp.concatenate([jnp.where(first, acc_ref[h], acc_ref[h + 1]) for h in heads[::2]], axis=-1)
    o_ref[0] = o * _silu(g_ref[0])


def _sb_prompt(q_bf, k_bf, v_bf, g_c, sb_bias):
    bsz, t, _ = q_bf.shape
    tq = min(SB_TILE, t)
    nq = t // tq
    width = SB_HEADS * HD_C
    groups = D_C // width
    bias_lanes = jnp.repeat(sb_bias.astype(F32), HD_C).reshape(groups, 1, width)
    ui = jnp.asarray(np.tril(np.ones((tq, tq), np.float32)), BF16)
    qspec = pl.BlockSpec((1, tq, width), lambda b, p, i: (b, i, p))
    kvspec = pl.BlockSpec((1, t, width), lambda b, p, i: (b, 0, p))
    return pl.pallas_call(
        _sb_prompt_body,
        grid=(bsz, groups, nq),
        in_specs=[qspec, kvspec, kvspec, qspec,
                  pl.BlockSpec((1, 1, width), lambda b, p, i: (p, 0, 0)),
                  pl.BlockSpec((tq, tq), lambda b, p, i: (0, 0))],
        out_specs=qspec,
        out_shape=jax.ShapeDtypeStruct((bsz, t, D_C), F32),
        scratch_shapes=[pltpu.VMEM((SB_HEADS, tq, LANES), F32), pltpu.VMEM((SB_HEADS, tq, 1), F32)],
        compiler_params=_cparams("arbitrary", "arbitrary", "arbitrary"),
        name="sb_prompt",
    )(q_bf, k_bf, v_bf, g_c, bias_lanes, ui)


def _sb_paged_body(pt_ref, qb_ref, g_ref, bias_ref, tri_ref, *refs):
    npg = PAGES_PER_STEP
    k_refs = refs[:npg]
    v_refs = refs[npg:2 * npg]
    o_ref = refs[2 * npg]
    acc, run = refs[2 * npg + 1:]
    j = pl.program_id(1)

    @pl.when(j == 0)
    def _():
        acc[...] = jnp.zeros_like(acc)
        run[...] = jnp.zeros_like(run)

    bias = bias_ref[:, 0:1]
    heads = range(H_C)
    for p in range(npg):
        z = jnp.concatenate([jnp.sum(k_refs[p][0, h] * qb_ref[0, h], axis=0, keepdims=True) for h in heads],
                            axis=0) + bias
        incl = _mm_sel(_softplus(z), tri_ref[...], 2)
        e = jnp.exp(z - incl - run[...])
        for h in heads:
            acc[h] += v_refs[p][0, h] * e[h:h + 1, :]
        run[...] += incl[:, 0:1]

    @pl.when(j == pl.num_programs(1) - 1)
    def _():
        ones = jnp.ones((8, PAGE), BF16)
        rows = []
        for h in heads:
            parts = [lax.dot_general(ones, part, (((1,), (1,)), ((), ())), preferred_element_type=F32)
                     for part in _split(acc[h], 3)]
            rows.append((parts[0] + parts[1] + parts[2])[0:1, :])
        o_ref[0] = jnp.concatenate(rows, axis=0) * _silu(g_ref[0])


def _sb_paged(q_s, g_s, cache_kt, cache_vt, page_table, sb_bias):
    bsz, n_pages = page_table.shape
    npg = PAGES_PER_STEP
    steps = n_pages // npg
    tri = jnp.asarray(np.tril(np.ones((PAGE, PAGE), np.float32)), BF16)
    bias = jnp.broadcast_to(sb_bias.astype(F32)[:, None], (H_C, LANES))
    qb = jnp.broadcast_to(q_s.astype(F32)[..., None], (bsz, H_C, HD_C, PAGE))

    def page_spec(p):
        return pl.BlockSpec((1, H_C, HD_C, PAGE),
                            lambda b, j, pt: (pt[b, n_pages - 1 - (j * npg + p)], 0, 0, 0))

    per_seq = pl.BlockSpec((1, H_C, HD_C), lambda b, j, pt: (b, 0, 0))
    full = lambda shape: pl.BlockSpec(shape, lambda b, j, pt: (0,) * len(shape))
    grid_spec = pltpu.PrefetchScalarGridSpec(
        num_scalar_prefetch=1,
        grid=(bsz, steps),
        in_specs=[pl.BlockSpec((1, H_C, HD_C, PAGE), lambda b, j, pt: (b, 0, 0, 0)), per_seq,
                  full((H_C, LANES)), full((PAGE, PAGE))] + [page_spec(p) for p in range(npg)] * 2,
        out_specs=per_seq,
        scratch_shapes=[pltpu.VMEM((H_C, HD_C, PAGE), F32), pltpu.VMEM((H_C, 1), F32)],
    )
    return pl.pallas_call(
        _sb_paged_body,
        grid_spec=grid_spec,
        out_shape=jax.ShapeDtypeStruct((bsz, H_C, HD_C), F32),
        compiler_params=_cparams("arbitrary", "arbitrary"),
        name="sb_paged",
    )(page_table, qb, g_s, bias, tri, *([cache_kt] * npg), *([cache_vt] * npg))


def _gla_body(q_ref, k_ref, v_ref, la_ref, gd_ref, s0_ref, dmat_ref, gng_ref, o_ref, sf_o, state):
    t_idx = pl.program_id(1)

    @pl.when(t_idx == 0)
    def _():
        state[...] = s0_ref[...]

    nseq, n, _ = q_ref.shape
    row = lax.broadcasted_iota(jnp.int32, (n, n), 0)
    col = lax.broadcasted_iota(jnp.int32, (n, n), 1)
    rsub = lax.broadcasted_iota(jnp.int32, (n, H_D * DK_D), 0)
    levels = _gla_levels(n)
    same_block = [(row >> int(math.log2(2 * m))) == (col >> int(math.log2(2 * m))) for m in levels]
    sls = [slice(DK_D * h, DK_D * (h + 1)) for h in range(H_D)]
    cast = lambda x: [x[:, sl].astype(BF16) for sl in sls]

    q_h, k_h, q_in, k_out, e_last, vh = [], [], [], [], [], []
    q_lv = [[] for _ in levels]
    k_lv = [[] for _ in levels]
    for s in range(nseq):
        q = q_ref[s]
        k = k_ref[s]
        dec = jnp.exp(_sel_mm(dmat_ref[...], la_ref[s], 3))
        blk = lambda idx: dec[n * idx:n * (idx + 1), :]
        q_h += cast(q)
        k_h += cast(k)
        q_in += cast(q * blk(0))
        k_out += cast(k * blk(1))
        e_last += [blk(0)[n - 1:n, sl] for sl in sls]
        for li, m in enumerate(levels):
            upper = (rsub & (2 * m - 1)) >= m
            e = blk(2 + li)
            q_lv[li] += cast(jnp.where(upper, q * e, 0.0))
            k_lv[li] += cast(jnp.where(upper, 0.0, k * e))
        v_all = v_ref[s]
        vh += [v_all[:, DV_D * h:DV_D * (h + 1)].astype(BF16) for h in range(H_D)]
    chains = range(nseq * H_D)
    st = [state[c // H_D, c % H_D] for c in chains]
    inter = [_mm_nt(q_in[c], st[c]) for c in chains]
    scores = [jnp.where(row == col, _mm_nt(q_h[c], k_h[c]), 0.0) for c in chains]
    for li in range(len(levels)):
        part = [_mm_nt(q_lv[li][c], k_lv[li][c]) for c in chains]
        scores = [scores[c] + jnp.where(same_block[li], part[c], 0.0) for c in chains]
    o = [inter[c] + _mm(scores[c], vh[c]) for c in chains]
    for c in chains:
        state[c // H_D, c % H_D] = st[c] * e_last[c] + _mm_tn(vh[c], k_out[c])
    outs = [oc * lax.rsqrt(jnp.mean(oc * oc, axis=-1, keepdims=True) + RMS_EPS) for oc in o]
    for s in range(nseq):
        o_ref[s] = jnp.concatenate(outs[s * H_D:(s + 1) * H_D], axis=-1) * gng_ref[...] * _silu(gd_ref[s])

    @pl.when(t_idx == pl.num_programs(1) - 1)
    def _():
        sf_o[...] = state[...]


def _gla_scan(q, k, v, la, gd, s0_t, gn_g, chunk):
    bsz, t, hk = q.shape
    nt = t // chunk
    nseq = GLA_SEQS_PER_STEP
    tok = lambda n: pl.BlockSpec((nseq, chunk, n), lambda b, i: (b, i, 0))
    st = pl.BlockSpec((nseq, H_D, DV_D, DK_D), lambda b, i: (b, 0, 0, 0))
    full = lambda shape: pl.BlockSpec(shape, lambda b, i: (0,) * len(shape))
    dmat = jnp.asarray(_gla_decay_rows(chunk), BF16)
    return pl.pallas_call(
        _gla_body,
        grid=(bsz // nseq, nt),
        in_specs=[tok(hk), tok(hk), tok(D_D), tok(hk), tok(D_D), st, full(dmat.shape), full((1, D_D))],
        out_specs=[tok(D_D), st],
        out_shape=[jax.ShapeDtypeStruct((bsz, t, D_D), F32),
                   jax.ShapeDtypeStruct((bsz, H_D, DV_D, DK_D), F32)],
        scratch_shapes=[pltpu.VMEM((nseq, H_D, DV_D, DK_D), F32)],
        compiler_params=_cparams("arbitrary", "arbitrary"),
        name="gla_scan_c%d" % chunk,
    )(q, k, v, la, gd, s0_t, dmat, gn_g.reshape(1, -1))


def _pad_tokens(a, n):
    return jnp.pad(a, ((0, 0), (0, n - a.shape[1]), (0, 0)))


def kernel(x_prompt, x_sample, state_rwkv, state_shift, state_s5_re, state_s5_im, cache_k, cache_v, state_gla,
           page_table, ln_g, ln_b, w_in_e, mu_shift, rwkv_w0, rwkv_w2, rwkv_a0, rwkv_a2, rwkv_k_k, rwkv_k_a,
           rwkv_r_k, rwkv_gn_g, rwkv_gn_b, s5_lambda_re, s5_lambda_im, s5_log_dt, s5_b_re, s5_b_im, s5_c_re,
           s5_c_im, s5_d, s5_glu_w, s5_glu_b, w_out_e, w_in_o, sb_bias, gla_alpha_w, gla_alpha_b, gla_gn_g,
           w_out_o):
    bp, t_p, _ = x_prompt.shape
    bs = x_sample.shape[0]
    n_phys = cache_k.shape[1]
    xs_rows = x_sample.reshape(1, bs, D_MODEL)

    w_in_bf = w_in_e[0].astype(BF16)
    prep_w = (w_in_bf, mu_shift[0], rwkv_w0[0], rwkv_w2[0], rwkv_a0[0], rwkv_a2[0], rwkv_k_k[0], rwkv_k_a[0])
    pe = _even_prep(x_prompt, jnp.zeros((bp, 1, N_SHIFT), F32), *prep_w, decode=False)
    se = _even_prep(xs_rows, state_shift[0].reshape(1, bs, N_SHIFT), *prep_w, decode=True)
    rw_w = (rwkv_gn_g[0], rwkv_gn_b[0], rwkv_r_k[0])
    oa_p, rw_p = _rwkv_scan(*pe[:7], jnp.zeros((bp, H_A, HD_A, HD_A), F32), *rw_w, chunk=RWKV_CHUNK)
    se_tok = [_pad_tokens(a.reshape(bs, 1, D_A), DECODE_CHUNK) for a in se[:7]]
    oa_s, rw_s = _rwkv_scan(*se_tok, state_rwkv[0], *rw_w, chunk=DECODE_CHUNK)

    abar_re, abar_im, bbr, bbi, cr, ci = _s5_params(s5_lambda_re[0], s5_lambda_im[0], s5_log_dt[0], s5_b_re[0],
                                                    s5_b_im[0], s5_c_re[0], s5_c_im[0])
    s5_tail = (cr.astype(BF16), ci.astype(BF16), s5_d[0], s5_glu_w[0].astype(BF16), s5_glu_b[0])
    zst = jnp.zeros((bp, 1, N_STATE_B), F32)
    ob_p, re_p, im_p = _s5_mixer(pe[7], pe[8], zst, zst, abar_re, abar_im, bbr.astype(BF16), bbi.astype(BF16),
                                 *s5_tail, decode=False)
    ob_s, re_s, im_s = _s5_mixer(se[7], se[8], state_s5_re[0].reshape(1, bs, N_STATE_B),
                                 state_s5_im[0].reshape(1, bs, N_STATE_B), abar_re, abar_im, bbr, bbi,
                                 *s5_tail, decode=True)

    w_out_e_bf = w_out_e[0].astype(BF16)
    w = w_in_o[0]
    w_o_bf = jnp.concatenate([w[:, :O_GD], w[:, O_GD + R_G:], w[:, O_GD:O_GD + R_G],
                              jnp.zeros((D_MODEL, N_PROJ_O_STAGED - w.shape[1]), F32)], axis=1).astype(BF16)
    alpha_w_pad = jnp.pad(gla_alpha_w[0], ((0, LANES - R_G), (0, 0)))
    odd_w = (w_out_e_bf, ln_g[0], ln_b[0], w_o_bf, alpha_w_pad, gla_alpha_b[0])
    po = _norm_odd_prep(oa_p, ob_p, x_prompt, *odd_w)
    so = _norm_odd_prep(oa_s[:, 0:1, :].reshape(1, bs, D_A), ob_s, xs_rows, *odd_w)
    x1_p, qc_p, kc_p, vc_p, kcb_p, vcb_p, gc_p, qd_p, kd_p, vd_p, la_p, gd_p = po
    x1_s, qc_s, kc_s, vc_s, _, _, gc_s, qd_s, kd_s, vd_s, la_s, gd_s = so

    oc_p = _sb_prompt(qc_p, kcb_p, vcb_p, gc_p, sb_bias[0])
    oc_s = _sb_paged(qc_s.reshape(bs, H_C, HD_C), gc_s.reshape(bs, H_C, HD_C),
                     jnp.transpose(cache_k[0], (0, 2, 3, 1)), jnp.transpose(cache_v[0], (0, 2, 3, 1)),
                     page_table, sb_bias[0])

    od_p, gl_p = _gla_scan(qd_p, kd_p, vd_p, la_p, gd_p, jnp.zeros((bp, H_D, DV_D, DK_D), F32), gla_gn_g[0],
                           chunk=GLA_CHUNK)
    tok_s = [_pad_tokens(a.reshape(bs, 1, a.shape[-1]), DECODE_CHUNK) for a in (qd_s, kd_s, vd_s, la_s, gd_s)]
    od_s, gl_s = _gla_scan(*tok_s, jnp.swapaxes(state_gla[0], -1, -2), gla_gn_g[0], chunk=DECODE_CHUNK)

    w_out_o_bf = w_out_o[0].astype(BF16)
    y_p = _out_norm(oc_p, od_p, x1_p, w_out_o_bf, ln_g[1], ln_b[1])
    y_s = _out_norm(oc_s.reshape(1, bs, D_C), od_s[:, 0:1, :].reshape(1, bs, D_D), x1_s, w_out_o_bf, ln_g[1],
                    ln_b[1])

    heads_c = lambda a, b, t: a.reshape(1, b, t, H_C, HD_C)
    return (y_p, y_s.reshape(bs, 1, D_MODEL),
            rw_p[None], rw_s[None],
            pe[9].reshape(1, bp, N_SHIFT), se[9].reshape(1, bs, N_SHIFT),
            re_p.reshape(1, bp, G_B, P_B), re_s.reshape(1, bs, G_B, P_B),
            im_p.reshape(1, bp, G_B, P_B), im_s.reshape(1, bs, G_B, P_B),
            heads_c(kc_p, bp, t_p), heads_c(kc_s, bs, 1), heads_c(vc_p, bp, t_p), heads_c(vc_s, bs, 1),
            jnp.swapaxes(gl_p, -1, -2)[None], jnp.swapaxes(gl_s, -1, -2)[None])
```
